```python
import jax, jax.numpy as jnp
from jax import lax
import numpy as np

D_MODEL = 2048
BATCH = 1
SEQ = 8192
DEPTH = 4

N_MEM = 256
D_MIX = 2 * D_MODEL
D_GROUP = D_MIX // 4
RW_HD = 64
RW_HEADS = D_GROUP // RW_HD
RW_DECAY_LORA = 64
RW_AAA_LORA = 64
RW_COLS = 3 * D_GROUP + RW_DECAY_LORA + RW_AAA_LORA
RW_GN_EPS = 64e-5
FX_HD = 64
FX_HEADS = D_GROUP // FX_HD
FX_BLOCK = 128
FX_COLS = 3 * D_GROUP + FX_HEADS
SSM_HD = 64
SSM_HEADS = D_GROUP // SSM_HD
SSM_GROUPS = 4
SSM_STATE = 128
SSM_CONV = 4
SSM_CHUNK = 128
SSM_CONV_DIM = D_GROUP + 2 * SSM_GROUPS * SSM_STATE
SSM_COLS = SSM_CONV_DIM + SSM_HEADS
MEM_HEADS = 4
MEM_HD = D_GROUP // MEM_HEADS

NORM_EPS = 1e-6
IN_SPLITS = (D_MIX, RW_COLS, FX_COLS, SSM_COLS, D_GROUP)
D_IN = sum(IN_SPLITS)
F32 = jnp.float32

kernel_name = "hybrid_rwkv7_fox_mamba2_memxattn"


def _split(t, sizes):
    return jnp.split(t, np.cumsum(sizes)[:-1].tolist(), axis=-1)


def _rmsnorm(t, w):
    tf = t.astype(F32)
    return tf * lax.rsqrt(jnp.mean(tf * tf, axis=-1, keepdims=True) + NORM_EPS) * w.astype(F32)


def _rwkv7_group(p, g, mu, w0, w_up, a0, a_up, k_k, k_a, r_k, lnx_w, lnx_b):
    bsz, seq = p.shape[0], p.shape[1]
    p = p.astype(F32)
    p_prev = jnp.pad(p, ((0, 0), (1, 0), (0, 0)))[:, :-1]
    p = p + (p_prev - p) * mu
    r, k, v, wd, ad = _split(p, (D_GROUP, D_GROUP, D_GROUP, RW_DECAY_LORA, RW_AAA_LORA))
    w = -jax.nn.softplus(-(w0 + jnp.tanh(wd) @ w_up)) - 0.5
    decay = jnp.exp(-jnp.exp(w))
    a = jax.nn.sigmoid(a0 + ad @ a_up)
    heads = lambda t: t.reshape(bsz, seq, RW_HEADS, RW_HD)
    kk = heads(k * k_k)
    kk = kk / jnp.maximum(jnp.sqrt(jnp.sum(kk * kk, axis=-1, keepdims=True)), 1e-12)
    k = k * (1.0 + (a - 1.0) * k_a)
    r, k, v, decay, a = heads(r), heads(k), heads(v), heads(decay), heads(a)
    b = kk * a

    def step(state, inp):
        r_t, w_t, k_t, v_t, kk_t, b_t = inp
        sa = jnp.einsum('bhvk,bhk->bhv', state, -kk_t)
        state = (state * w_t[:, :, None, :] + sa[..., None] * b_t[:, :, None, :]
                 + v_t[..., None] * k_t[:, :, None, :])
        return state, jnp.einsum('bhvk,bhk->bhv', state, r_t)

    xs = tuple(jnp.moveaxis(t, 1, 0) for t in (r, decay, k, v, kk, b))
    _, y = lax.scan(step, jnp.zeros((bsz, RW_HEADS, RW_HD, RW_HD), F32), xs)
    y = jnp.moveaxis(y, 0, 1)
    mean = jnp.mean(y, axis=-1, keepdims=True)
    var = jnp.mean(jnp.square(y - mean), axis=-1, keepdims=True)
    y = ((y - mean) * lax.rsqrt(var + RW_GN_EPS)).reshape(bsz, seq, D_GROUP) * lnx_w + lnx_b
    bonus = jnp.sum(r * k * r_k, axis=-1, keepdims=True) * v
    y = y + bonus.reshape(bsz, seq, D_GROUP)
    return y * jax.nn.silu(g.astype(F32))


def _fox_group(p, g, f_bias, q_norm_w, k_norm_w):
    bsz, seq = p.shape[0], p.shape[1]
    q, k, v, f = _split(p, (D_GROUP, D_GROUP, D_GROUP, FX_HEADS))
    q = jnp.transpose(_rmsnorm(q.reshape(bsz, seq, FX_HEADS, FX_HD), q_norm_w), (0, 2, 1, 3))
    k = jnp.transpose(_rmsnorm(k.reshape(bsz, seq, FX_HEADS, FX_HD), k_norm_w), (0, 2, 1, 3))
    v = jnp.transpose(v.reshape(bsz, seq, FX_HEADS, FX_HD).astype(F32), (0, 2, 1, 3))
    log_f = jax.nn.log_sigmoid(f.astype(F32) + f_bias)
    c = jnp.transpose(jnp.cumsum(log_f, axis=1), (0, 2, 1))
    scale = FX_HD ** -0.5
    key_pos = jnp.arange(seq)

    def block(i):
        start = i * FX_BLOCK
        qb = lax.dynamic_slice_in_dim(q, start, FX_BLOCK, axis=2)
        cb = lax.dynamic_slice_in_dim(c, start, FX_BLOCK, axis=2)
        s = jnp.einsum('bhqd,bhkd->bhqk', qb, k) * scale + cb[..., None] - c[:, :, None, :]
        q_pos = start + jnp.arange(FX_BLOCK)
        s = jnp.where(q_pos[:, None] >= key_pos[None, :], s, -jnp.inf)
        return jnp.einsum('bhqk,bhkd->bhqd', jax.nn.softmax(s, axis=-1), v)

    o = lax.map(block, jnp.arange(seq // FX_BLOCK))
    o = jnp.transpose(o, (1, 0, 3, 2, 4)).reshape(bsz, seq, D_GROUP)
    return o * jax.nn.silu(g.astype(F32))


def _mamba2_group(p, z, conv_w, conv_b, dt_bias, a_log, d_skip, norm_w):
    bsz, seq = p.shape[0], p.shape[1]
    xbc, dt = _split(p.astype(F32), (SSM_CONV_DIM, SSM_HEADS))
    xbc = lax.conv_general_dilated(xbc, conv_w.astype(F32)[:, None, :], window_strides=(1,),
                                   padding=[(SSM_CONV - 1, 0)],
                                   dimension_numbers=('NWC', 'WIO', 'NWC'),
                                   feature_group_count=SSM_CONV_DIM)
    xbc = jax.nn.silu(xbc + conv_b)
    xs, bm, cm = _split(xbc, (D_GROUP, SSM_GROUPS * SSM_STATE, SSM_GROUPS * SSM_STATE))
    rep = SSM_HEADS // SSM_GROUPS
    x_h = xs.reshape(bsz, seq, SSM_HEADS, SSM_HD)
    bm = jnp.repeat(bm.reshape(bsz, seq, SSM_GROUPS, SSM_STATE), rep, axis=2)
    cm = jnp.repeat(cm.reshape(bsz, seq, SSM_GROUPS, SSM_STATE), rep, axis=2)
    dt = jax.nn.softplus(dt + dt_bias)
    da = dt * (-jnp.exp(a_log.astype(F32)))
    xdt = x_h * dt[..., None]
    n_chunks = seq // SSM_CHUNK
    chunk = lambda t: t.reshape((bsz, n_chunks, SSM_CHUNK) + t.shape[2:])
    xc, bc, cc = chunk(xdt), chunk(bm), chunk(cm)
    acs = jnp.cumsum(jnp.moveaxis(chunk(da), 3, 1), axis=-1)
    tril = jnp.tril(jnp.ones((SSM_CHUNK, SSM_CHUNK), bool))
    lmat = jnp.exp(jnp.where(tril, acs[..., :, None] - acs[..., None, :], -jnp.inf))
    cb = jnp.einsum('bclhn,bcshn->bhcls', cc, bc)
    y_diag = jnp.einsum('bhcls,bcshp->bclhp', cb * lmat, xc)
    decay_states = jnp.exp(acs[..., -1:] - acs)
    states = jnp.einsum('bclhn,bhcl,bclhp->bchpn', bc, decay_states, xc)
    chunk_decay = jnp.exp(acs[..., -1])

    def step(h, inp):
        st, dec = inp
        return h * dec[:, :, None, None] + st, h

    _, prev = lax.scan(step, jnp.zeros((bsz, SSM_HEADS, SSM_HD, SSM_STATE), F32),
                       (jnp.moveaxis(states, 1, 0), jnp.moveaxis(chunk_decay, 2, 0)))
    prev = jnp.moveaxis(prev, 0, 1)
    y_off = jnp.einsum('bclhn,bchpn,bhcl->bclhp', cc, prev, jnp.exp(acs))
    y = (y_diag + y_off).reshape(bsz, seq, SSM_HEADS, SSM_HD) + d_skip[:, None] * x_h
    y = y.reshape(bsz, seq, D_GROUP) * jax.nn.silu(z.astype(F32))
    yg = y.reshape(bsz, seq, SSM_GROUPS, D_GROUP // SSM_GROUPS)
    yg = yg * lax.rsqrt(jnp.mean(yg * yg, axis=-1, keepdims=True) + NORM_EPS)
    return yg.reshape(bsz, seq, D_GROUP) * norm_w


def _memory_group(q, g, mem, mem_norm_w, w_kv, q_norm_w, k_norm_w):
    bsz, seq = q.shape[0], q.shape[1]
    n_mem = mem.shape[1]
    q = _rmsnorm(q.reshape(bsz, seq, MEM_HEADS, MEM_HD), q_norm_w)
    m = _rmsnorm(mem, mem_norm_w).astype(mem.dtype)
    mk, mv = jnp.split(m @ w_kv, 2, axis=-1)
    mk = _rmsnorm(mk.reshape(bsz, n_mem, MEM_HEADS, MEM_HD), k_norm_w)
    mv = mv.reshape(bsz, n_mem, MEM_HEADS, MEM_HD).astype(F32)
    s = jnp.einsum('bshd,bmhd->bhsm', q, mk) * (MEM_HD ** -0.5)
    o = jnp.einsum('bhsm,bmhd->bshd', jax.nn.softmax(s, axis=-1), mv)
    return o.reshape(bsz, seq, D_GROUP) * jax.nn.silu(g.astype(F32))


def setup_inputs(seed: int = 0) -> dict:
    key = jax.random.key(seed)
    ks = jax.random.split(key, 32)
    n = lambda i, shape: jax.random.normal(ks[i], shape, F32)
    u = lambda i, shape, lo, hi: jax.random.uniform(ks[i], shape, F32, lo, hi)
    dt0 = jnp.exp(u(20, (DEPTH, SSM_HEADS), float(np.log(1e-3)), float(np.log(1e-1))))
    return {
        "x": n(0, (BATCH, SEQ, D_MODEL)),
        "mem": n(1, (BATCH, N_MEM, D_MODEL)),
        "norm_w": 1.0 + 0.02 * n(2, (DEPTH, D_MODEL)),
        "w_in": n(3, (DEPTH, D_MODEL, D_IN)) * D_MODEL ** -0.5,
        "rw_mu": u(4, (DEPTH, RW_COLS), 0.0, 1.0),
        "rw_w0": u(5, (DEPTH, D_GROUP), -6.5, -1.5),
        "rw_w_up": n(6, (DEPTH, RW_DECAY_LORA, D_GROUP)) * 0.1 * RW_DECAY_LORA ** -0.5,
        "rw_a0": 0.1 * n(7, (DEPTH, D_GROUP)),
        "rw_a_up": n(8, (DEPTH, RW_AAA_LORA, D_GROUP)) * 0.1 * RW_AAA_LORA ** -0.5,
        "rw_k_k": 0.85 + 0.05 * n(9, (DEPTH, D_GROUP)),
        "rw_k_a": 1.0 + 0.05 * n(10, (DEPTH, D_GROUP)),
        "rw_r_k": 0.1 * n(11, (DEPTH, RW_HEADS, RW_HD)),
        "rw_lnx_w": 1.0 + 0.02 * n(12, (DEPTH, D_GROUP)),
        "rw_lnx_b": 0.02 * n(13, (DEPTH, D_GROUP)),
        "fx_f_bias": u(14, (DEPTH, FX_HEADS), 1.0, 4.0),
        "fx_q_norm_w": 1.0 + 0.02 * n(15, (DEPTH, FX_HD)),
        "fx_k_norm_w": 1.0 + 0.02 * n(16, (DEPTH, FX_HD)),
        "ssm_conv_w": 0.5 * n(17, (DEPTH, SSM_CONV, SSM_CONV_DIM)),
        "ssm_conv_b": 0.02 * n(18, (DEPTH, SSM_CONV_DIM)),
        "ssm_dt_bias": dt0 + jnp.log(-jnp.expm1(-dt0)),
        "ssm_a_log": jnp.log(u(19, (DEPTH, SSM_HEADS), 1.0, 16.0)),
        "ssm_d": 1.0 + 0.02 * n(21, (DEPTH, SSM_HEADS)),
        "ssm_norm_w": 1.0 + 0.02 * n(22, (DEPTH, D_GROUP)),
        "mem_norm_w": 1.0 + 0.02 * n(23, (DEPTH, D_MODEL)),
        "mem_w_kv": n(24, (DEPTH, D_MODEL, 2 * D_GROUP)) * D_MODEL ** -0.5,
        "mem_q_norm_w": 1.0 + 0.02 * n(25, (DEPTH, MEM_HD)),
        "mem_k_norm_w": 1.0 + 0.02 * n(26, (DEPTH, MEM_HD)),
        "w_out": n(27, (DEPTH, D_MIX, D_MODEL)) * D_MIX ** -0.5 / np.sqrt(2.0 * DEPTH),
    }


def reference(x, mem, norm_w, w_in, rw_mu, rw_w0, rw_w_up, rw_a0, rw_a_up, rw_k_k, rw_k_a,
              rw_r_k, rw_lnx_w, rw_lnx_b, fx_f_bias, fx_q_norm_w, fx_k_norm_w, ssm_conv_w,
              ssm_conv_b, ssm_dt_bias, ssm_a_log, ssm_d, ssm_norm_w, mem_norm_w, mem_w_kv,
              mem_q_norm_w, mem_k_norm_w, w_out):
    for l in range(DEPTH):
        h = _rmsnorm(x, norm_w[l]).astype(x.dtype)
        proj = h @ w_in[l]
        gate, p_rw, p_fx, p_ssm, q_mem = _split(proj, IN_SPLITS)
        g_rw, g_fx, z_ssm, g_mem = jnp.split(gate, 4, axis=-1)
        y_rw = _rwkv7_group(p_rw, g_rw, rw_mu[l], rw_w0[l], rw_w_up[l], rw_a0[l], rw_a_up[l],
                            rw_k_k[l], rw_k_a[l], rw_r_k[l], rw_lnx_w[l], rw_lnx_b[l])
        y_fx = _fox_group(p_fx, g_fx, fx_f_bias[l], fx_q_norm_w[l], fx_k_norm_w[l])
        y_ssm = _mamba2_group(p_ssm, z_ssm, ssm_conv_w[l], ssm_conv_b[l], ssm_dt_bias[l],
                              ssm_a_log[l], ssm_d[l], ssm_norm_w[l])
        y_mem = _memory_group(q_mem, g_mem, mem, mem_norm_w[l], mem_w_kv[l],
                              mem_q_norm_w[l], mem_k_norm_w[l])
        y = jnp.concatenate([y_rw, y_fx, y_ssm, y_mem], axis=-1).astype(x.dtype)
        x = x + y @ w_out[l]
    return x
```

```python
import functools

import jax
import jax.numpy as jnp
from jax import lax
from jax.experimental import pallas as pl
from jax.experimental.pallas import tpu as pltpu

F32 = jnp.float32
BF16 = jnp.bfloat16

D_MODEL = 2048
DEPTH = 4
D_MIX = 2 * D_MODEL
D_GROUP = D_MIX // 4
HD = 64
N_HEADS = D_GROUP // HD
RW_LORA = 64
RW_COLS = 3 * D_GROUP + 2 * RW_LORA
RW_GN_EPS = 64e-5
FX_COLS = 3 * D_GROUP + N_HEADS
SSM_GROUPS = 4
SSM_STATE = 128
SSM_CONV = 4
SSM_CONV_DIM = D_GROUP + 2 * SSM_GROUPS * SSM_STATE
SSM_COLS = SSM_CONV_DIM + N_HEADS
MEM_HEADS = 4
MEM_HD = D_GROUP // MEM_HEADS
NORM_EPS = 1e-6
LANES = 128

RW_CHUNK = 64
VMEM_LIMIT = 48 * 1024 * 1024


def _cparams(*sem):
    return pltpu.CompilerParams(dimension_semantics=sem, vmem_limit_bytes=VMEM_LIMIT)


def _dot(a, b):
    return jnp.dot(a.astype(BF16), b.astype(BF16), preferred_element_type=F32)


def _dot_nt(a, b):
    return lax.dot_general(a.astype(BF16), b.astype(BF16), (((1,), (1,)), ((), ())),
                           preferred_element_type=F32)


def _dot_tn(a, b):
    return lax.dot_general(a.astype(BF16), b.astype(BF16), (((0,), (0,)), ((), ())),
                           preferred_element_type=F32)


def _split3(x):
    hi = x.astype(BF16)
    r1 = x - hi.astype(F32)
    mid = r1.astype(BF16)
    lo = (r1 - mid.astype(F32)).astype(BF16)
    return hi, mid, lo


def _dot_exact_lhs(a_bf16, x):
    hi, mid, lo = _split3(x)
    d = lambda p: jnp.dot(a_bf16, p, preferred_element_type=F32)
    return d(hi) + (d(mid) + d(lo))


def _dot_exact_rhs(x, b_bf16):
    hi, mid, lo = _split3(x)
    d = lambda p: jnp.dot(p, b_bf16, preferred_element_type=F32)
    return d(hi) + (d(mid) + d(lo))


def _silu(x):
    return x * (1.0 / (1.0 + jnp.exp(-x)))


def _softplus(x):
    return jnp.maximum(x, 0.0) + jnp.log(1.0 + jnp.exp(-jnp.abs(x)))


def _tri_incl(n):
    r = lax.broadcasted_iota(jnp.int32, (n, n), 0)
    c = lax.broadcasted_iota(jnp.int32, (n, n), 1)
    return r >= c


def _rmsnorm_kernel(x_ref, w_ref, o_ref):
    x = x_ref[...]
    ms = jnp.mean(x * x, axis=-1, keepdims=True)
    o_ref[...] = (x * lax.rsqrt(ms + NORM_EPS) * w_ref[...]).astype(o_ref.dtype)


def _rmsnorm_rows(x, w, tm):
    m, d = x.shape
    return pl.pallas_call(
        _rmsnorm_kernel,
        grid=(m // tm,),
        in_specs=[pl.BlockSpec((tm, d), lambda i: (i, 0)), pl.BlockSpec((1, d), lambda i: (0, 0))],
        out_specs=pl.BlockSpec((tm, d), lambda i: (i, 0)),
        out_shape=jax.ShapeDtypeStruct((m, d), BF16),
        compiler_params=_cparams("parallel"),
        name="rmsnorm_rows",
    )(x, w.reshape(1, d))


def _matmul_kernel(a_ref, b_ref, o_ref):
    o_ref[...] = jnp.dot(a_ref[...], b_ref[...], preferred_element_type=F32).astype(o_ref.dtype)


def _matmul(a, b, tm, tn, out_dtype=F32):
    m, k = a.shape
    n = b.shape[1]
    return pl.pallas_call(
        _matmul_kernel,
        grid=(n // tn, m // tm),
        in_specs=[pl.BlockSpec((tm, k), lambda j, i: (i, 0)), pl.BlockSpec((k, tn), lambda j, i: (0, j))],
        out_specs=pl.BlockSpec((tm, tn), lambda j, i: (i, j)),
        out_shape=jax.ShapeDtypeStruct((m, n), out_dtype),
        compiler_params=_cparams("parallel", "parallel"),
        name="matmul",
    )(a, b)


def _out_proj_kernel(x_ref, y0_ref, y1_ref, y2_ref, y3_ref, w_ref, o_ref):
    acc = x_ref[...]
    for g, y_ref in enumerate((y0_ref, y1_ref, y2_ref, y3_ref)):
        acc = acc + jnp.dot(y_ref[...], w_ref[g * D_GROUP:(g + 1) * D_GROUP, :], preferred_element_type=F32)
    o_ref[...] = acc


def _out_proj(x, ys, w_out, tm, tn):
    m, d = x.shape
    yspec = pl.BlockSpec((tm, D_GROUP), lambda j, i: (i, 0))
    return pl.pallas_call(
        _out_proj_kernel,
        grid=(d // tn, m // tm),
        in_specs=[pl.BlockSpec((tm, tn), lambda j, i: (i, j)), yspec, yspec, yspec, yspec,
                  pl.BlockSpec((D_MIX, tn), lambda j, i: (0, j))],
        out_specs=pl.BlockSpec((tm, tn), lambda j, i: (i, j)),
        out_shape=jax.ShapeDtypeStruct((m, d), F32),
        compiler_params=_cparams("parallel", "parallel"),
        name="out_proj",
    )(x, *ys, w_out)


def _seg_sum(x, seg_ref):
    hi = x.astype(BF16)
    lo = (x - hi.astype(F32)).astype(BF16)
    e = seg_ref[...]
    return jnp.dot(hi, e, preferred_element_type=F32) + jnp.dot(lo, e, preferred_element_type=F32)


def _rw_prep_kernel(r_ref, k_ref, v_ref, lo_ref, prev_ref, mu_ref, w0_ref, wup_ref, a0_ref, aup_ref,
                    kk_ref, ka_ref, rk_ref, seg_ref,
                    ro_ref, lwo_ref, ko_ref, vo_ref, kko_ref, bo_ref, bonus_ref):
    i = pl.program_id(0)
    t = r_ref.shape[0]
    row = lax.broadcasted_iota(jnp.int32, (t, 1), 0)
    first = i == 0

    def shifted(cur, c0, c1):
        last = prev_ref[7:8, c0:c1]
        last = jnp.where(first, jnp.zeros_like(last), last)
        prev = jnp.where(row == 0, last, pltpu.roll(cur, 1, axis=0))
        return cur + (prev - cur) * mu_ref[:, c0:c1]

    g = D_GROUP
    r = shifted(r_ref[...], 0, g)
    k = shifted(k_ref[...], g, 2 * g)
    v = shifted(v_ref[...], 2 * g, 3 * g)
    lora = shifted(lo_ref[...], 3 * g, 3 * g + 2 * RW_LORA)

    w = -_softplus(-(w0_ref[...] + _dot(jnp.tanh(lora), wup_ref[...]))) - 0.5
    lw = -jnp.exp(w)
    a = 1.0 / (1.0 + jnp.exp(-(a0_ref[...] + _dot(lora, aup_ref[...]))))
    kk = k * kk_ref[...]
    kk = kk / jnp.maximum(jnp.sqrt(_seg_sum(kk * kk, seg_ref)), 1e-12)
    k2 = k * (1.0 + (a - 1.0) * ka_ref[...])
    ro_ref[...] = r
    lwo_ref[...] = lw
    ko_ref[...] = k2
    vo_ref[...] = v
    kko_ref[...] = kk
    bo_ref[...] = kk * a
    bonus_ref[...] = _seg_sum(r * k2 * rk_ref[...], seg_ref) * v


def _rw_scan_kernel(r_ref, lw_ref, k_ref, v_ref, kk_ref, b_ref, y_ref, state_ref):
    L = r_ref.shape[0]

    @pl.when(pl.program_id(0) == 0)
    def _():
        state_ref[...] = jnp.zeros_like(state_ref)

    lw = lw_ref[...]
    tri = _tri_incl(L)
    c = _dot_exact_lhs(tri.astype(BF16), lw)
    e_pos = jnp.exp(c)
    e_neg = jnp.exp(-c)
    a_hat = -kk_ref[...] * jnp.exp(c - lw)
    b_hat = b_ref[...] * e_neg
    k_hat = k_ref[...] * e_neg
    r_hat = r_ref[...] * e_pos
    v = v_ref[...]
    p_last = e_pos[L - 1:L, :]

    rr = lax.broadcasted_iota(jnp.int32, (2 * L, 2 * L), 0)
    cc = lax.broadcasted_iota(jnp.int32, (2 * L, 2 * L), 1) & (L - 1)
    lower = cc < (rr & (L - 1)) + jnp.where(rr >= L, 1, 0)
    eye = (lax.broadcasted_iota(jnp.int32, (L, L), 0) == lax.broadcasted_iota(jnp.int32, (L, L), 1)).astype(F32)

    for h in range(N_HEADS):
        sl = slice(h * HD, (h + 1) * HD)
        ah, bh, kh, rh, vh = a_hat[:, sl], b_hat[:, sl], k_hat[:, sl], r_hat[:, sl], v[:, sl]
        x = jnp.concatenate([ah, rh], axis=0)
        y = jnp.concatenate([bh, kh], axis=0)
        amat = jnp.where(lower, _dot_nt(x, y), 0.0)
        n = amat[:L, :L]
        tinv = eye + n
        p = n
        for _ in range(5):
            p = _dot(p, p)
            tinv = tinv + _dot(tinv, p)
        s0 = state_ref[h]
        u = _dot(tinv, _dot_nt(ah, s0) + _dot(amat[:L, L:], vh))
        uv = jnp.concatenate([u, vh], axis=0)
        y_ref[:, sl] = _dot_nt(rh, s0) + _dot(amat[L:, :], uv)
        state_ref[h] = (s0 + _dot_tn(uv, y)) * p_last[:, sl]


def _rw_post_kernel(y_ref, bonus_ref, g_ref, lw_ref, lb_ref, seg_ref, o_ref):
    y = y_ref[...]
    mean = _seg_sum(y, seg_ref) * (1.0 / HD)
    d = y - mean
    var = _seg_sum(d * d, seg_ref) * (1.0 / HD)
    yn = d * lax.rsqrt(var + RW_GN_EPS) * lw_ref[...] + lb_ref[...] + bonus_ref[...]
    o_ref[...] = (yn * _silu(g_ref[...])).astype(o_ref.dtype)


def _rwkv_group(p_rw, gate, prm, seg, tm):
    s = p_rw.shape[0]
    g = D_GROUP
    nb = g // LANES
    row = lambda v: v.reshape(1, -1)
    wup = jnp.concatenate([prm["w_up"], jnp.zeros_like(prm["w_up"])], axis=0).astype(BF16)
    aup = jnp.concatenate([jnp.zeros_like(prm["a_up"]), prm["a_up"]], axis=0).astype(BF16)
    full = lambda shape: pl.BlockSpec(shape, lambda i: (0, 0))
    col = lambda j: pl.BlockSpec((tm, g), lambda i, j=j: (i, j))
    outs = pl.pallas_call(
        _rw_prep_kernel,
        grid=(s // tm,),
        in_specs=[col(0), col(1), col(2),
                  pl.BlockSpec((tm, LANES), lambda i: (i, 3 * nb)),
                  pl.BlockSpec((8, RW_COLS), lambda i: (jnp.maximum(i * (tm // 8) - 1, 0), 0)),
                  full((1, RW_COLS)), full((1, g)), full((LANES, g)), full((1, g)), full((LANES, g)),
                  full((1, g)), full((1, g)), full((1, g)), full((g, g))],
        out_specs=[pl.BlockSpec((tm, g), lambda i: (i, 0))] * 7,
        out_shape=[jax.ShapeDtypeStruct((s, g), F32)] * 7,
        compiler_params=_cparams("parallel"),
        name="rwkv_prep",
    )(p_rw, p_rw, p_rw, p_rw, p_rw, row(prm["mu"]), row(prm["w0"]), wup, row(prm["a0"]), aup,
      row(prm["k_k"]), row(prm["k_a"]), row(prm["r_k"]), seg)
    r, lw, k2, v, kk, b, bonus = outs
    L = RW_CHUNK
    blk = pl.BlockSpec((L, g), lambda i: (i, 0))
    y = pl.pallas_call(
        _rw_scan_kernel,
        grid=(s // L,),
        in_specs=[blk] * 6,
        out_specs=blk,
        out_shape=jax.ShapeDtypeStruct((s, g), F32),
        scratch_shapes=[pltpu.VMEM((N_HEADS, HD, HD), F32)],
        compiler_params=_cparams("arbitrary"),
        name="rwkv_scan",
    )(r, lw, k2, v, kk, b)
    rowblk = pl.BlockSpec((tm, g), lambda i: (i, 0))
    return pl.pallas_call(
        _rw_post_kernel,
        grid=(s // tm,),
        in_specs=[rowblk, rowblk, pl.BlockSpec((tm, g), lambda i: (i, 0)), full((1, g)), full((1, g)), full((g, g))],
        out_specs=rowblk,
        out_shape=jax.ShapeDtypeStruct((s, g), BF16),
        compiler_params=_cparams("parallel"),
        name="rwkv_post",
    )(y, bonus, gate, row(prm["lnx_w"]), row(prm["lnx_b"]), seg)


def _fx_prep_kernel(q_ref, k_ref, v_ref, f_ref, qw_ref, kw_ref, fb_ref, seg_ref,
                    qo_ref, ko_ref, vo_ref, c_ref, carry_ref):
    t = q_ref.shape[0]

    @pl.when(pl.program_id(0) == 0)
    def _():
        carry_ref[...] = jnp.zeros_like(carry_ref)

    def headnorm(x, w):
        ms = _seg_sum(x * x, seg_ref) * (1.0 / HD)
        return x * lax.rsqrt(ms + NORM_EPS) * w

    qo_ref[...] = (headnorm(q_ref[...], qw_ref[...]) * (HD ** -0.5)).astype(qo_ref.dtype)
    ko_ref[...] = headnorm(k_ref[...], kw_ref[...]).astype(ko_ref.dtype)
    vo_ref[...] = v_ref[...].astype(vo_ref.dtype)
    z = f_ref[...] + fb_ref[...]
    log_f = jnp.minimum(z, 0.0) - jnp.log(1.0 + jnp.exp(-jnp.abs(z)))
    c = _dot_exact_lhs(_tri_incl(t).astype(BF16), log_f) + carry_ref[...]
    c_ref[...] = c
    carry_ref[...] = c[t - 1:t, :]


def _fx_attn_kernel(q_ref, k_ref, v_ref, cq_ref, ck_ref, g_ref, o_ref, *, tk):
    i = pl.program_id(1)
    tq = q_ref.shape[0]
    lane = lax.broadcasted_iota(jnp.int32, (1, LANES), 1)
    q = q_ref[...]
    causal = (lax.broadcasted_iota(jnp.int32, (tq, tk), 0) >= lax.broadcasted_iota(jnp.int32, (tq, tk), 1))
    outs = []
    for e in range(2):
        mine = (lane >= HD) if e else (lane < HD)
        qe = jnp.where(mine, q, jnp.zeros_like(q))
        cq = cq_ref[0, :, e:e + 1]

        def scores(j):
            kj = k_ref[pl.ds(j * tk, tk), :]
            s = lax.dot_general(qe, kj, (((1,), (1,)), ((), ())), preferred_element_type=F32)
            return s + (cq - ck_ref[0, e:e + 1, pl.ds(j * tk, tk)])

        def update(carry, s, j):
            m, l, acc = carry
            m_new = jnp.maximum(m, jnp.max(s, axis=-1, keepdims=True))
            alpha = jnp.exp(m - m_new)
            p = jnp.exp(s - m_new)
            l = alpha * l + jnp.sum(p, axis=-1, keepdims=True)
            acc = alpha * acc + jnp.dot(p.astype(BF16), v_ref[pl.ds(j * tk, tk), :], preferred_element_type=F32)
            return m_new, l, acc

        def body(j, carry):
            return update(carry, scores(j), j)

        init = (jnp.full((tq, 1), -1e30, F32), jnp.zeros((tq, 1), F32), jnp.zeros((tq, LANES), F32))
        carry = lax.fori_loop(0, i, body, init)
        s_diag = jnp.where(causal, scores(i), -1e30)
        m, l, acc = update(carry, s_diag, i)
        outs.append(acc / l)
    o = jnp.where(lane < HD, outs[0], outs[1])
    o_ref[...] = (o * _silu(g_ref[...])).astype(o_ref.dtype)


def _fox_group(p_fx, gate, prm, seg, tm, tq):
    s = p_fx.shape[0]
    g = D_GROUP
    nb = g // LANES
    full = lambda shape: pl.BlockSpec(shape, lambda i: (0, 0))
    col = lambda j: pl.BlockSpec((tm, g), lambda i, j=j: (i, j))
    rowblk = pl.BlockSpec((tm, g), lambda i: (i, 0))
    tile_w = lambda w: jnp.tile(w, N_HEADS).reshape(1, g)
    fb = jnp.zeros((1, LANES), F32).at[0, :N_HEADS].set(prm["f_bias"])
    qn, kn, vb, c = pl.pallas_call(
        _fx_prep_kernel,
        grid=(s // tm,),
        in_specs=[col(0), col(1), col(2), pl.BlockSpec((tm, LANES), lambda i: (i, 3 * nb)),
                  full((1, g)), full((1, g)), full((1, LANES)), full((g, g))],
        out_specs=[rowblk, rowblk, rowblk, pl.BlockSpec((tm, LANES), lambda i: (i, 0))],
        out_shape=[jax.ShapeDtypeStruct((s, g), BF16)] * 3 + [jax.ShapeDtypeStruct((s, LANES), F32)],
        scratch_shapes=[pltpu.VMEM((1, LANES), F32)],
        compiler_params=_cparams("arbitrary"),
        name="fox_prep",
    )(p_fx, p_fx, p_fx, p_fx, tile_w(prm["q_norm_w"]), tile_w(prm["k_norm_w"]), fb, seg)
    ch = c[:, :N_HEADS]
    cq = ch.reshape(s, N_HEADS // 2, 2).transpose(1, 0, 2)
    ck = cq.transpose(0, 2, 1)
    return pl.pallas_call(
        functools.partial(_fx_attn_kernel, tk=tq),
        grid=(N_HEADS // 2, s // tq),
        in_specs=[pl.BlockSpec((tq, LANES), lambda j, i: (i, j)),
                  pl.BlockSpec((s, LANES), lambda j, i: (0, j)),
                  pl.BlockSpec((s, LANES), lambda j, i: (0, j)),
                  pl.BlockSpec((1, tq, 2), lambda j, i: (j, i, 0)),
                  pl.BlockSpec((1, 2, s), lambda j, i: (j, 0, 0)),
                  pl.BlockSpec((tq, LANES), lambda j, i: (i, nb + j))],
        out_specs=pl.BlockSpec((tq, LANES), lambda j, i: (i, j)),
        out_shape=jax.ShapeDtypeStruct((s, g), BF16),
        compiler_params=_cparams("parallel", "parallel"),
        name="fox_attn",
    )(qn, kn, vb, cq, ck, gate)


def _ssd_kernel(xbc_ref, dt_ref, z_ref, cw_ref, cb_ref, dtb_ref, alog_ref, dskip_ref, nw_ref,
                o_ref, tail_ref, state_ref):
    t = xbc_ref.shape[0]
    g = D_GROUP
    n = SSM_STATE

    @pl.when(pl.program_id(0) == 0)
    def _():
        tail_ref[...] = jnp.zeros_like(tail_ref)
        state_ref[...] = jnp.zeros_like(state_ref)

    cur = xbc_ref[...]
    tail = tail_ref[...]
    row8 = lax.broadcasted_iota(jnp.int32, (8, 1), 0)
    conv = cur * cw_ref[SSM_CONV - 1:SSM_CONV, :] + cb_ref[...]
    for sft in range(1, SSM_CONV):
        rolled = pltpu.roll(cur, sft, axis=0)
        head = jnp.where(row8 < sft, pltpu.roll(tail, sft, axis=0), rolled[:8])
        shifted = jnp.concatenate([head, rolled[8:]], axis=0)
        conv = conv + shifted * cw_ref[SSM_CONV - 1 - sft:SSM_CONV - sft, :]
    tail_ref[...] = cur[t - 8:, :]
    xbc = _silu(conv)
    xs = xbc[:, :g]
    dt = _softplus(dt_ref[...] + dtb_ref[...])
    da = dt * (-jnp.exp(alog_ref[...]))
    tri = _tri_incl(t)
    acs = _dot_exact_lhs(tri.astype(BF16), da)
    acs_t = acs.T
    seg_decay = jnp.exp(acs[t - 1:t, :] - acs)
    e_acs = jnp.exp(acs)
    tile_decay = e_acs[t - 1:t, :]
    ys = []
    for grp in range(SSM_GROUPS):
        bm = xbc[:, g + grp * n:g + (grp + 1) * n]
        cm = xbc[:, g + SSM_GROUPS * n + grp * n:g + SSM_GROUPS * n + (grp + 1) * n]
        cb = _dot_nt(cm, bm)
        for hh in range(N_HEADS // SSM_GROUPS):
            h = grp * (N_HEADS // SSM_GROUPS) + hh
            xh = xs[:, h * HD:(h + 1) * HD]
            xdt = xh * dt[:, h:h + 1]
            lmat = jnp.exp(jnp.where(tri, acs[:, h:h + 1] - acs_t[h:h + 1, :], -jnp.inf))
            st = state_ref[h]
            y = _dot(cb * lmat, xdt) + _dot_nt(cm, st) * e_acs[:, h:h + 1] + dskip_ref[:, h:h + 1] * xh
            ys.append(y)
            state_ref[h] = st * tile_decay[:, h:h + 1] + _dot_tn(xdt * seg_decay[:, h:h + 1], bm)
    y = jnp.concatenate(ys, axis=1) * _silu(z_ref[...])
    gw = g // SSM_GROUPS
    outs = []
    for grp in range(SSM_GROUPS):
        yg = y[:, grp * gw:(grp + 1) * gw]
        outs.append(yg * lax.rsqrt(jnp.mean(yg * yg, axis=-1, keepdims=True) + NORM_EPS))
    o_ref[...] = (jnp.concatenate(outs, axis=1) * nw_ref[...]).astype(o_ref.dtype)


def _ssd_group(p_ssm, gate, prm, t):
    s = p_ssm.shape[0]
    g = D_GROUP
    full = lambda shape: pl.BlockSpec(shape, lambda i: (0, 0))
    pad_heads = lambda v: jnp.zeros((1, LANES), F32).at[0, :N_HEADS].set(v)
    return pl.pallas_call(
        _ssd_kernel,
        grid=(s // t,),
        in_specs=[pl.BlockSpec((t, SSM_CONV_DIM), lambda i: (i, 0)),
                  pl.BlockSpec((t, LANES), lambda i: (i, SSM_CONV_DIM // LANES)),
                  pl.BlockSpec((t, g), lambda i: (i, 2)),
                  full((SSM_CONV, SSM_CONV_DIM)), full((1, SSM_CONV_DIM)), full((1, LANES)), full((1, LANES)),
                  full((1, LANES)), full((1, g))],
        out_specs=pl.BlockSpec((t, g), lambda i: (i, 0)),
        out_shape=jax.ShapeDtypeStruct((s, g), BF16),
        scratch_shapes=[pltpu.VMEM((8, SSM_CONV_DIM), F32), pltpu.VMEM((N_HEADS, HD, SSM_STATE), F32)],
        compiler_params=_cparams("arbitrary"),
        name="ssd_scan",
    )(p_ssm, p_ssm, gate, prm["conv_w"], prm["conv_b"].reshape(1, -1), pad_heads(prm["dt_bias"]),
      pad_heads(prm["a_log"]), pad_heads(prm["d"]), prm["norm_w"].reshape(1, -1))


def _mem_attn_kernel(q_ref, g_ref, kv_ref, qw_ref, kw_ref, o_ref):
    outs = []
    for h in range(MEM_HEADS):
        sl = slice(h * MEM_HD, (h + 1) * MEM_HD)

        def norm(x, w):
            return x * lax.rsqrt(jnp.mean(x * x, axis=-1, keepdims=True) + NORM_EPS) * w

        q = norm(q_ref[:, sl], qw_ref[...]) * (MEM_HD ** -0.5)
        k = norm(kv_ref[:, sl], kw_ref[...])
        v = kv_ref[:, D_GROUP + h * MEM_HD:D_GROUP + (h + 1) * MEM_HD]
        s = _dot_nt(q, k)
        p = jnp.exp(s - jnp.max(s, axis=-1, keepdims=True))
        outs.append(_dot(p, v) / jnp.sum(p, axis=-1, keepdims=True))
    o_ref[...] = (jnp.concatenate(outs, axis=1) * _silu(g_ref[...])).astype(o_ref.dtype)


def _mem_group(q_mem, gate, mem2d, prm, tm):
    s = q_mem.shape[0]
    g = D_GROUP
    n_mem = mem2d.shape[0]
    m = _rmsnorm_rows(mem2d, prm["norm_w"], n_mem)
    kv = _matmul(m, prm["w_kv"].astype(BF16), n_mem, 1024)
    full = lambda shape: pl.BlockSpec(shape, lambda i: (0, 0))
    return pl.pallas_call(
        _mem_attn_kernel,
        grid=(s // tm,),
        in_specs=[pl.BlockSpec((tm, g), lambda i: (i, 0)), pl.BlockSpec((tm, g), lambda i: (i, 3)),
                  full((n_mem, 2 * g)), full((1, MEM_HD)), full((1, MEM_HD))],
        out_specs=pl.BlockSpec((tm, g), lambda i: (i, 0)),
        out_shape=jax.ShapeDtypeStruct((s, g), BF16),
        compiler_params=_cparams("parallel"),
        name="mem_attn",
    )(q_mem, gate, kv, prm["q_norm_w"].reshape(1, -1), prm["k_norm_w"].reshape(1, -1))


def _pad_cols(w, n):
    return jnp.pad(w, ((0, 0), (0, n - w.shape[1])))


def _tile_rows(s, pref):
    t = min(pref, s)
    assert s % t == 0
    return t


def _layer(x, mem2d, lp, seg):
    s = x.shape[0]
    tm = _tile_rows(s, 512)
    w_in = lp["w_in"]
    o_rw = D_MIX
    o_fx = o_rw + RW_COLS
    o_ssm = o_fx + FX_COLS
    o_q = o_ssm + SSM_COLS
    w_gate = w_in[:, :o_rw].astype(BF16)
    w_rw = w_in[:, o_rw:o_fx].astype(BF16)
    w_fx = _pad_cols(w_in[:, o_fx:o_ssm], 3 * D_GROUP + LANES).astype(BF16)
    w_ssm = _pad_cols(w_in[:, o_ssm:o_q], SSM_CONV_DIM + LANES).astype(BF16)
    w_q = w_in[:, o_q:].astype(BF16)

    h = _rmsnorm_rows(x, lp["norm_w"], tm)
    gate = _matmul(h, w_gate, tm, 1024)
    p_rw = _matmul(h, w_rw, tm, 640)
    p_fx = _matmul(h, w_fx, tm, 640)
    p_ssm = _matmul(h, w_ssm, tm, w_ssm.shape[1])
    q_mem = _matmul(h, w_q, tm, 1024)

    y_rw = _rwkv_group(p_rw, gate, lp["rw"], seg, _tile_rows(s, 256))
    y_fx = _fox_group(p_fx, gate, lp["fx"], seg, _tile_rows(s, 256), _tile_rows(s, 512))
    y_ssm = _ssd_group(p_ssm, gate, lp["ssm"], _tile_rows(s, 256))
    y_mem = _mem_group(q_mem, gate, mem2d, lp["mem"], tm)
    return _out_proj(x, (y_rw, y_fx, y_ssm, y_mem), lp["w_out"].astype(BF16), tm, 1024)


def kernel(x, mem, norm_w, w_in, rw_mu, rw_w0, rw_w_up, rw_a0, rw_a_up, rw_k_k, rw_k_a, rw_r_k, rw_lnx_w, rw_lnx_b, fx_f_bias, fx_q_norm_w, fx_k_norm_w, ssm_conv_w, ssm_conv_b, ssm_dt_bias, ssm_a_log, ssm_d, ssm_norm_w, mem_norm_w, mem_w_kv, mem_q_norm_w, mem_k_norm_w, w_out):
    assert x.shape[0] == 1 and mem.shape[0] == 1
    x2 = x[0]
    mem2d = mem[0]
    head_of = jnp.arange(D_GROUP) // HD
    seg = (head_of[:, None] == head_of[None, :]).astype(BF16)
    for l in range(DEPTH):
        lp = {
            "norm_w": norm_w[l], "w_in": w_in[l], "w_out": w_out[l],
            "rw": {"mu": rw_mu[l], "w0": rw_w0[l], "w_up": rw_w_up[l], "a0": rw_a0[l], "a_up": rw_a_up[l],
                   "k_k": rw_k_k[l], "k_a": rw_k_a[l], "r_k": rw_r_k[l], "lnx_w": rw_lnx_w[l], "lnx_b": rw_lnx_b[l]},
            "fx": {"f_bias": fx_f_bias[l], "q_norm_w": fx_q_norm_w[l], "k_norm_w": fx_k_norm_w[l]},
            "ssm": {"conv_w": ssm_conv_w[l], "conv_b": ssm_conv_b[l], "dt_bias": ssm_dt_bias[l], "a_log": ssm_a_log[l],
                    "d": ssm_d[l], "norm_w": ssm_norm_w[l]},
            "mem": {"norm_w": mem_norm_w[l], "w_kv": mem_w_kv[l], "q_norm_w": mem_q_norm_w[l],
                    "k_norm_w": mem_k_norm_w[l]},
        }
        x2 = _layer(x2, mem2d, lp, seg)
    return x2[None]
```

```python
import functools

import jax
import jax.numpy as jnp
from jax import lax
from jax.experimental import pallas as pl
from jax.experimental.pallas import tpu as pltpu

F32 = jnp.float32
BF16 = jnp.bfloat16

D_MODEL = 2048
DEPTH = 4
D_MIX = 2 * D_MODEL
D_GROUP = D_MIX // 4
HD = 64
N_HEADS = D_GROUP // HD
RW_LORA = 64
RW_COLS = 3 * D_GROUP + 2 * RW_LORA
RW_GN_EPS = 64e-5
FX_COLS = 3 * D_GROUP + N_HEADS
SSM_GROUPS = 4
SSM_STATE = 128
SSM_CONV = 4
SSM_CONV_DIM = D_GROUP + 2 * SSM_GROUPS * SSM_STATE
SSM_COLS = SSM_CONV_DIM + N_HEADS
MEM_HEADS = 4
MEM_HD = D_GROUP // MEM_HEADS
NORM_EPS = 1e-6
LANES = 128

RW_CHUNK = 64
LOG2E = 1.4426950408889634
FX_AUG = 3
FX_VROWS = HD + 16
RW_PACK = 4
VMEM_LIMIT = 48 * 1024 * 1024


def _cparams(*sem):
    return pltpu.CompilerParams(dimension_semantics=sem, vmem_limit_bytes=VMEM_LIMIT)


def _dot(a, b):
    return jnp.dot(a.astype(BF16), b.astype(BF16), preferred_element_type=F32)


def _dot_nt(a, b):
    return lax.dot_general(a.astype(BF16), b.astype(BF16), (((1,), (1,)), ((), ())),
                           preferred_element_type=F32)


def _dot_tn(a, b):
    return lax.dot_general(a.astype(BF16), b.astype(BF16), (((0,), (0,)), ((), ())),
                           preferred_element_type=F32)


def _split3(x):
    hi = x.astype(BF16)
    r1 = x - hi.astype(F32)
    mid = r1.astype(BF16)
    lo = (r1 - mid.astype(F32)).astype(BF16)
    return hi, mid, lo


def _dot_exact_lhs(a_bf16, x):
    hi, mid, lo = _split3(x)
    d = lambda p: jnp.dot(a_bf16, p, preferred_element_type=F32)
    return d(hi) + (d(mid) + d(lo))


def _dot_exact_rhs(x, b_bf16):
    hi, mid, lo = _split3(x)
    d = lambda p: jnp.dot(p, b_bf16, preferred_element_type=F32)
    return d(hi) + (d(mid) + d(lo))


def _silu(x):
    return x * (1.0 / (1.0 + jnp.exp(-x)))


def _softplus(x):
    return jnp.maximum(x, 0.0) + jnp.log(1.0 + jnp.exp(-jnp.abs(x)))


def _tri_incl(n):
    r = lax.broadcasted_iota(jnp.int32, (n, n), 0)
    c = lax.broadcasted_iota(jnp.int32, (n, n), 1)
    return r >= c


def _rmsnorm_kernel(x_ref, w_ref, o_ref):
    x = x_ref[...]
    ms = jnp.mean(x * x, axis=-1, keepdims=True)
    o_ref[...] = (x * lax.rsqrt(ms + NORM_EPS) * w_ref[...]).astype(o_ref.dtype)


def _rmsnorm_rows(x, w, tm):
    m, d = x.shape
    return pl.pallas_call(
        _rmsnorm_kernel,
        grid=(m // tm,),
        in_specs=[pl.BlockSpec((tm, d), lambda i: (i, 0)), pl.BlockSpec((1, d), lambda i: (0, 0))],
        out_specs=pl.BlockSpec((tm, d), lambda i: (i, 0)),
        out_shape=jax.ShapeDtypeStruct((m, d), BF16),
        compiler_params=_cparams("parallel"),
        name="rmsnorm_rows",
    )(x, w.reshape(1, d))


def _matmul_kernel(a_ref, b_ref, o_ref):
    o_ref[...] = jnp.dot(a_ref[...], b_ref[...], preferred_element_type=F32).astype(o_ref.dtype)


def _matmul(a, b, tm, tn, out_dtype=F32):
    m, k = a.shape
    n = b.shape[1]
    return pl.pallas_call(
        _matmul_kernel,
        grid=(n // tn, m // tm),
        in_specs=[pl.BlockSpec((tm, k), lambda j, i: (i, 0)), pl.BlockSpec((k, tn), lambda j, i: (0, j))],
        out_specs=pl.BlockSpec((tm, tn), lambda j, i: (i, j)),
        out_shape=jax.ShapeDtypeStruct((m, n), out_dtype),
        compiler_params=_cparams("parallel", "parallel"),
        name="matmul",
    )(a, b)


def _out_proj_kernel(x_ref, y0_ref, y1_ref, y2_ref, y3_ref, w_ref, o_ref):
    acc = x_ref[...]
    for g, y_ref in enumerate((y0_ref, y1_ref, y2_ref, y3_ref)):
        acc = acc + jnp.dot(y_ref[...], w_ref[g * D_GROUP:(g + 1) * D_GROUP, :], preferred_element_type=F32)
    o_ref[...] = acc


def _out_proj(x, ys, w_out, tm, tn):
    m, d = x.shape
    yspec = pl.BlockSpec((tm, D_GROUP), lambda j, i: (i, 0))
    return pl.pallas_call(
        _out_proj_kernel,
        grid=(d // tn, m // tm),
        in_specs=[pl.BlockSpec((tm, tn), lambda j, i: (i, j)), yspec, yspec, yspec, yspec,
                  pl.BlockSpec((D_MIX, tn), lambda j, i: (0, j))],
        out_specs=pl.BlockSpec((tm, tn), lambda j, i: (i, j)),
        out_shape=jax.ShapeDtypeStruct((m, d), F32),
        compiler_params=_cparams("parallel", "parallel"),
        name="out_proj",
    )(x, *ys, w_out)


def _seg_sum(x, seg_ref):
    hi = x.astype(BF16)
    lo = (x - hi.astype(F32)).astype(BF16)
    e = seg_ref[...]
    return jnp.dot(hi, e, preferred_element_type=F32) + jnp.dot(lo, e, preferred_element_type=F32)


def _rw_prep_kernel(r_ref, k_ref, v_ref, lo_ref, prev_ref, mu_ref, w0_ref, wup_ref, a0_ref, aup_ref,
                    kk_ref, ka_ref, rk_ref, seg_ref,
                    ro_ref, lwo_ref, ko_ref, vo_ref, kko_ref, bo_ref, bonus_ref):
    i = pl.program_id(0)
    t = r_ref.shape[0]
    row = lax.broadcasted_iota(jnp.int32, (t, 1), 0)
    first = i == 0

    def shifted(cur, c0, c1):
        last = prev_ref[7:8, c0:c1]
        last = jnp.where(first, jnp.zeros_like(last), last)
        prev = jnp.where(row == 0, last, pltpu.roll(cur, 1, axis=0))
        return cur + (prev - cur) * mu_ref[:, c0:c1]

    g = D_GROUP
    r = shifted(r_ref[...], 0, g)
    k = shifted(k_ref[...], g, 2 * g)
    v = shifted(v_ref[...], 2 * g, 3 * g)
    lora = shifted(lo_ref[...], 3 * g, 3 * g + 2 * RW_LORA)

    w = -_softplus(-(w0_ref[...] + _dot(jnp.tanh(lora), wup_ref[...]))) - 0.5
    lw = -jnp.exp(w)
    a = 1.0 / (1.0 + jnp.exp(-(a0_ref[...] + _dot(lora, aup_ref[...]))))
    kk = k * kk_ref[...]
    kk = kk / jnp.maximum(jnp.sqrt(_seg_sum(kk * kk, seg_ref)), 1e-12)
    k2 = k * (1.0 + (a - 1.0) * ka_ref[...])
    ro_ref[...] = r
    lwo_ref[...] = lw
    ko_ref[...] = k2
    vo_ref[...] = v
    kko_ref[...] = kk
    bo_ref[...] = kk * a
    bonus_ref[...] = _seg_sum(r * k2 * rk_ref[...], seg_ref) * v


def _rw_scan_kernel(r_ref, lw_ref, k_ref, v_ref, kk_ref, b_ref, y_ref, state_ref):
    L = r_ref.shape[0]

    @pl.when(pl.program_id(0) == 0)
    def _():
        state_ref[...] = jnp.zeros_like(state_ref)

    lw = lw_ref[...]
    tri = _tri_incl(L)
    c = _dot_exact_lhs(tri.astype(BF16), lw)
    e_pos = jnp.exp(c)
    e_neg = jnp.exp(-c)
    a_hat = -kk_ref[...] * jnp.exp(c - lw)
    b_hat = b_ref[...] * e_neg
    k_hat = k_ref[...] * e_neg
    r_hat = r_ref[...] * e_pos
    v = v_ref[...]
    p_last = e_pos[L - 1:L, :]

    W = RW_PACK * L
    ri = lax.broadcasted_iota(jnp.int32, (W, W), 0)
    ci = lax.broadcasted_iota(jnp.int32, (W, W), 1)
    same = (ri // L) == (ci // L)
    strict = jnp.logical_and(same, (ri & (L - 1)) > (ci & (L - 1)))
    incl = jnp.logical_and(same, (ri & (L - 1)) >= (ci & (L - 1)))
    eye = (ri == ci).astype(F32)
    tile = lambda t: jnp.concatenate([t] * RW_PACK, axis=0)
    own = lambda t: jnp.where(same, tile(t), 0.0)

    packs = range(N_HEADS // RW_PACK)
    each = lambda f, *cols: [f(*(c[g] for c in cols)) for g in packs]
    cat = lambda *ts: jnp.concatenate(ts, axis=0)
    sls = [slice(g * W, (g + 1) * W) for g in packs]
    vw = [own(v[:, sl]).astype(BF16) for sl in sls]
    yb = [tile(b_hat[:, sl]) for sl in sls]
    yk = [tile(k_hat[:, sl]) for sl in sls]
    xar = [cat(own(a_hat[:, sl]), own(r_hat[:, sl])).astype(BF16) for sl in sls]
    ab = each(_dot_nt, xar, yb)
    ak = each(_dot_nt, xar, yk)
    n = each(lambda t: jnp.where(strict, t[:W], 0.0), ab)
    a_k = each(lambda t: cat(jnp.where(strict, t[:W], 0.0), jnp.where(incl, t[W:], 0.0)).astype(BF16), ak)
    a_rb = each(lambda t: jnp.where(incl, t[W:], 0.0).astype(BF16), ab)
    s0 = [state_ref[g] for g in packs]
    xs = each(_dot_nt, xar, s0)
    av = each(_dot, a_k, vw)
    tinv = each(lambda t: eye + t, n)
    p = each(_dot, n, n)
    for _ in range(4):
        pt = each(lambda pp, tt: _dot(cat(pp, tt), pp), p, tinv)
        p = each(lambda t: t[:W], pt)
        tinv = each(lambda tt, t: tt + t[W:], tinv, pt)
    tinv = each(lambda tt, pp: tt + _dot(tt, pp), tinv, p)
    u = each(lambda tt, x, a: _dot(tt, x[:W] + a[:W]).astype(BF16), tinv, xs, av)
    yw = each(lambda x, a, arb, uu: x[W:] + a[W:] + _dot(arb, uu), xs, av, a_rb, u)
    upd = each(lambda uu, vv, b, k: _dot_tn(cat(uu, vv), cat(jnp.where(same, b, 0.0), jnp.where(same, k, 0.0))),
               u, vw, yb, yk)
    for g in packs:
        y_ref[:, sls[g]] = functools.reduce(jnp.add, [yw[g][i * L:(i + 1) * L] for i in range(RW_PACK)])
        state_ref[g] = (s0[g] + upd[g]) * p_last[:, sls[g]]


def _rw_post_kernel(y_ref, bonus_ref, g_ref, lw_ref, lb_ref, seg_ref, o_ref):
    y = y_ref[...]
    mean = _seg_sum(y, seg_ref) * (1.0 / HD)
    d = y - mean
    var = _seg_sum(d * d, seg_ref) * (1.0 / HD)
    yn = d * lax.rsqrt(var + RW_GN_EPS) * lw_ref[...] + lb_ref[...] + bonus_ref[...]
    o_ref[...] = (yn * _silu(g_ref[...])).astype(o_ref.dtype)


def _rwkv_group(p_rw, gate, prm, seg, tm):
    s = p_rw.shape[0]
    g = D_GROUP
    nb = g // LANES
    row = lambda v: v.reshape(1, -1)
    wup = jnp.concatenate([prm["w_up"], jnp.zeros_like(prm["w_up"])], axis=0).astype(BF16)
    aup = jnp.concatenate([jnp.zeros_like(prm["a_up"]), prm["a_up"]], axis=0).astype(BF16)
    full = lambda shape: pl.BlockSpec(shape, lambda i: (0, 0))
    col = lambda j: pl.BlockSpec((tm, g), lambda i, j=j: (i, j))
    outs = pl.pallas_call(
        _rw_prep_kernel,
        grid=(s // tm,),
        in_specs=[col(0), col(1), col(2),
                  pl.BlockSpec((tm, LANES), lambda i: (i, 3 * nb)),
                  pl.BlockSpec((8, RW_COLS), lambda i: (jnp.maximum(i * (tm // 8) - 1, 0), 0)),
                  full((1, RW_COLS)), full((1, g)), full((LANES, g)), full((1, g)), full((LANES, g)),
                  full((1, g)), full((1, g)), full((1, g)), full((g, g))],
        out_specs=[pl.BlockSpec((tm, g), lambda i: (i, 0))] * 7,
        out_shape=[jax.ShapeDtypeStruct((s, g), F32)] * 7,
        compiler_params=_cparams("parallel"),
        name="rwkv_prep",
    )(p_rw, p_rw, p_rw, p_rw, p_rw, row(prm["mu"]), row(prm["w0"]), wup, row(prm["a0"]), aup,
      row(prm["k_k"]), row(prm["k_a"]), row(prm["r_k"]), seg)
    r, lw, k2, v, kk, b, bonus = outs
    L = RW_CHUNK
    blk = pl.BlockSpec((L, g), lambda i: (i, 0))
    y = pl.pallas_call(
        _rw_scan_kernel,
        grid=(s // L,),
        in_specs=[blk] * 6,
        out_specs=blk,
        out_shape=jax.ShapeDtypeStruct((s, g), F32),
        scratch_shapes=[pltpu.VMEM((N_HEADS // RW_PACK, RW_PACK * HD, RW_PACK * HD), F32)],
        compiler_params=_cparams("arbitrary"),
        name="rwkv_scan",
    )(r, lw, k2, v, kk, b)
    rowblk = pl.BlockSpec((tm, g), lambda i: (i, 0))
    return pl.pallas_call(
        _rw_post_kernel,
        grid=(s // tm,),
        in_specs=[rowblk, rowblk, pl.BlockSpec((tm, g), lambda i: (i, 0)), full((1, g)), full((1, g)), full((g, g))],
        out_specs=rowblk,
        out_shape=jax.ShapeDtypeStruct((s, g), BF16),
        compiler_params=_cparams("parallel"),
        name="rwkv_post",
    )(y, bonus, gate, row(prm["lnx_w"]), row(prm["lnx_b"]), seg)


def _fx_prep_kernel(q_ref, k_ref, v_ref, f_ref, qw_ref, kw_ref, fb_ref, seg_ref,
                    qo_ref, ko_ref, vo_ref, carry_ref):
    t = q_ref.shape[0]

    @pl.when(pl.program_id(0) == 0)
    def _():
        carry_ref[...] = jnp.zeros_like(carry_ref)

    def headnorm(x, w):
        ms = _seg_sum(x * x, seg_ref) * (1.0 / HD)
        return x * lax.rsqrt(ms + NORM_EPS) * w

    qn = headnorm(q_ref[...], qw_ref[...]) * (HD ** -0.5 * LOG2E)
    kn = headnorm(k_ref[...], kw_ref[...])
    z = f_ref[...] + fb_ref[...]
    log_f = jnp.minimum(z, 0.0) - jnp.log(1.0 + jnp.exp(-jnp.abs(z)))
    c = _dot_exact_lhs(_tri_incl(t).astype(BF16), log_f) + carry_ref[...]
    carry_ref[...] = c[t - 1:t, :]
    vt = v_ref[...].T.astype(vo_ref.dtype)
    lane = lax.broadcasted_iota(jnp.int32, (t, LANES), 1)
    for h in range(N_HEADS):
        pair = slice((h // 2) * LANES, (h // 2 + 1) * LANES)
        own = (lane >= HD) if h % 2 else (lane < HD)
        aug0 = 0 if h % 2 else HD
        hi, mid, lo = (piece.astype(F32) for piece in _split3(jnp.broadcast_to(c[:, h:h + 1] * (-LOG2E), (t, LANES))))
        k_aug = jnp.where(lane == aug0, hi, jnp.where(lane == aug0 + 1, mid, jnp.where(lane == aug0 + 2, lo, 0.0)))
        q_aug = jnp.where(lane < aug0 + FX_AUG, 1.0, 0.0)
        ko_ref[h] = jnp.where(own, kn[:, pair], k_aug).astype(BF16)
        qo_ref[h] = jnp.where(own, qn[:, pair], q_aug).astype(BF16)
        vo_ref[h, :HD, :] = vt[h * HD:(h + 1) * HD, :]
        vo_ref[h, HD:, :] = jnp.ones((FX_VROWS - HD, t), vo_ref.dtype)


def _fx_attn_kernel(q_ref, k_ref, vt_ref, g_ref, o_ref, *, tk):
    i = pl.program_id(1)
    tq = q_ref.shape[1]
    heads = range(2)
    q = [q_ref[e] for e in heads]
    key_le_query = (lax.broadcasted_iota(jnp.int32, (tk, tq), 0) <= lax.broadcasted_iota(jnp.int32, (tk, tq), 1))

    def tile(j, carry, diagonal):
        start = pl.multiple_of(j * tk, tk)
        s = [lax.dot_general(k_ref[e, pl.ds(start, tk), :], q[e], (((1,), (1,)), ((), ())),
                             preferred_element_type=F32) for e in heads]
        if diagonal:
            s = [jnp.where(key_le_query, t, -1e30) for t in s]
        m_new = [jnp.maximum(carry[e][0], jnp.max(s[e], axis=0, keepdims=True)) for e in heads]
        p = [jnp.exp2(s[e] - m_new[e]).astype(BF16) for e in heads]
        pv = [jnp.dot(vt_ref[e, :, pl.ds(start, tk)], p[e], preferred_element_type=F32) for e in heads]
        return tuple((m_new[e], jnp.exp2(carry[e][0] - m_new[e]) * carry[e][1] + pv[e]) for e in heads)

    init = tuple((jnp.full((1, tq), -1e30, F32), jnp.zeros((FX_VROWS, tq), F32)) for _ in heads)
    carry = lax.fori_loop(0, i, lambda j, c: tile(j, c, False), init)
    carry = tile(i, carry, True)
    ot = jnp.concatenate([carry[e][1][:HD] / carry[e][1][HD:HD + 1] for e in heads], axis=0)
    o_ref[...] = (ot.T * _silu(g_ref[...])).astype(o_ref.dtype)


def _fox_group(p_fx, gate, prm, seg, tm, tq):
    s = p_fx.shape[0]
    g = D_GROUP
    nb = g // LANES
    full = lambda shape: pl.BlockSpec(shape, lambda i: (0, 0))
    col = lambda j: pl.BlockSpec((tm, g), lambda i, j=j: (i, j))
    rowblk = pl.BlockSpec((tm, g), lambda i: (i, 0))
    tile_w = lambda w: jnp.tile(w, N_HEADS).reshape(1, g)
    fb = jnp.zeros((1, LANES), F32).at[0, :N_HEADS].set(prm["f_bias"])
    q_aug, k_aug, vt_aug = pl.pallas_call(
        _fx_prep_kernel,
        grid=(s // tm,),
        in_specs=[col(0), col(1), col(2), pl.BlockSpec((tm, LANES), lambda i: (i, 3 * nb)),
                  full((1, g)), full((1, g)), full((1, LANES)), full((g, g))],
        out_specs=[pl.BlockSpec((N_HEADS, tm, LANES), lambda i: (0, i, 0)),
                   pl.BlockSpec((N_HEADS, tm, LANES), lambda i: (0, i, 0)),
                   pl.BlockSpec((N_HEADS, FX_VROWS, tm), lambda i: (0, 0, i))],
        out_shape=[jax.ShapeDtypeStruct((N_HEADS, s, LANES), BF16)] * 2
        + [jax.ShapeDtypeStruct((N_HEADS, FX_VROWS, s), BF16)],
        scratch_shapes=[pltpu.VMEM((1, LANES), F32)],
        compiler_params=_cparams("arbitrary"),
        name="fox_prep",
    )(p_fx, p_fx, p_fx, p_fx, tile_w(prm["q_norm_w"]), tile_w(prm["k_norm_w"]), fb, seg)
    return pl.pallas_call(
        functools.partial(_fx_attn_kernel, tk=tq),
        grid=(N_HEADS // 2, s // tq),
        in_specs=[pl.BlockSpec((2, tq, LANES), lambda j, i: (j, i, 0)),
                  pl.BlockSpec((2, s, LANES), lambda j, i: (j, 0, 0)),
                  pl.BlockSpec((2, FX_VROWS, s), lambda j, i: (j, 0, 0)),
                  pl.BlockSpec((tq, LANES), lambda j, i: (i, nb + j))],
        out_specs=pl.BlockSpec((tq, LANES), lambda j, i: (i, j)),
        out_shape=jax.ShapeDtypeStruct((s, g), BF16),
        compiler_params=_cparams("parallel", "parallel"),
        name="fox_attn",
    )(q_aug, k_aug, vt_aug, gate)


def _ssd_kernel(xbc_ref, dt_ref, z_ref, cw_ref, cb_ref, dtb_ref, alog_ref, dskip_ref, nw_ref,
                o_ref, tail_ref, state_ref):
    t = xbc_ref.shape[0]
    g = D_GROUP
    n = SSM_STATE

    @pl.when(pl.program_id(0) == 0)
    def _():
        tail_ref[...] = jnp.zeros_like(tail_ref)
        state_ref[...] = jnp.zeros_like(state_ref)

    cur = xbc_ref[...]
    tail = tail_ref[...]
    row8 = lax.broadcasted_iota(jnp.int32, (8, 1), 0)
    conv = cur * cw_ref[SSM_CONV - 1:SSM_CONV, :] + cb_ref[...]
    for sft in range(1, SSM_CONV):
        rolled = pltpu.roll(cur, sft, axis=0)
        head = jnp.where(row8 < sft, pltpu.roll(tail, sft, axis=0), rolled[:8])
        shifted = jnp.concatenate([head, rolled[8:]], axis=0)
        conv = conv + shifted * cw_ref[SSM_CONV - 1 - sft:SSM_CONV - sft, :]
    tail_ref[...] = cur[t - 8:, :]
    xbc = _silu(conv)
    xs = xbc[:, :g]
    dt = _softplus(dt_ref[...] + dtb_ref[...])
    da = dt * (-jnp.exp(alog_ref[...]))
    tri = _tri_incl(t)
    acs = _dot_exact_lhs(tri.astype(BF16), da)
    acs_t = acs.T
    seg_decay = jnp.exp(acs[t - 1:t, :] - acs)
    e_acs = jnp.exp(acs)
    tile_decay = e_acs[t - 1:t, :]
    ys = []
    for grp in range(SSM_GROUPS):
        bm = xbc[:, g + grp * n:g + (grp + 1) * n]
        cm = xbc[:, g + SSM_GROUPS * n + grp * n:g + SSM_GROUPS * n + (grp + 1) * n]
        cb = _dot_nt(cm, bm)
        for hh in range(N_HEADS // SSM_GROUPS):
            h = grp * (N_HEADS // SSM_GROUPS) + hh
            xh = xs[:, h * HD:(h + 1) * HD]
            xdt = xh * dt[:, h:h + 1]
            lmat = jnp.exp(jnp.where(tri, acs[:, h:h + 1] - acs_t[h:h + 1, :], -jnp.inf))
            st = state_ref[h]
            y = _dot(cb * lmat, xdt) + _dot_nt(cm, st) * e_acs[:, h:h + 1] + dskip_ref[:, h:h + 1] * xh
            ys.append(y)
            state_ref[h] = st * tile_decay[:, h:h + 1] + _dot_tn(xdt * seg_decay[:, h:h + 1], bm)
    y = jnp.concatenate(ys, axis=1) * _silu(z_ref[...])
    gw = g // SSM_GROUPS
    outs = []
    for grp in range(SSM_GROUPS):
        yg = y[:, grp * gw:(grp + 1) * gw]
        outs.append(yg * lax.rsqrt(jnp.mean(yg * yg, axis=-1, keepdims=True) + NORM_EPS))
    o_ref[...] = (jnp.concatenate(outs, axis=1) * nw_ref[...]).astype(o_ref.dtype)


def _ssd_group(p_ssm, gate, prm, t):
    s = p_ssm.shape[0]
    g = D_GROUP
    full = lambda shape: pl.BlockSpec(shape, lambda i: (0, 0))
    pad_heads = lambda v: jnp.zeros((1, LANES), F32).at[0, :N_HEADS].set(v)
    return pl.pallas_call(
        _ssd_kernel,
        grid=(s // t,),
        in_specs=[pl.BlockSpec((t, SSM_CONV_DIM), lambda i: (i, 0)),
                  pl.BlockSpec((t, LANES), lambda i: (i, SSM_CONV_DIM // LANES)),
                  pl.BlockSpec((t, g), lambda i: (i, 2)),
                  full((SSM_CONV, SSM_CONV_DIM)), full((1, SSM_CONV_DIM)), full((1, LANES)), full((1, LANES)),
                  full((1, LANES)), full((1, g))],
        out_specs=pl.BlockSpec((t, g), lambda i: (i, 0)),
        out_shape=jax.ShapeDtypeStruct((s, g), BF16),
        scratch_shapes=[pltpu.VMEM((8, SSM_CONV_DIM), F32), pltpu.VMEM((N_HEADS, HD, SSM_STATE), F32)],
        compiler_params=_cparams("arbitrary"),
        name="ssd_scan",
    )(p_ssm, p_ssm, gate, prm["conv_w"], prm["conv_b"].reshape(1, -1), pad_heads(prm["dt_bias"]),
      pad_heads(prm["a_log"]), pad_heads(prm["d"]), prm["norm_w"].reshape(1, -1))


def _mem_attn_kernel(q_ref, g_ref, kv_ref, qw_ref, kw_ref, o_ref):
    outs = []
    for h in range(MEM_HEADS):
        sl = slice(h * MEM_HD, (h + 1) * MEM_HD)

        def norm(x, w):
            return x * lax.rsqrt(jnp.mean(x * x, axis=-1, keepdims=True) + NORM_EPS) * w

        q = norm(q_ref[:, sl], qw_ref[...]) * (MEM_HD ** -0.5)
        k = norm(kv_ref[:, sl], kw_ref[...])
        v = kv_ref[:, D_GROUP + h * MEM_HD:D_GROUP + (h + 1) * MEM_HD]
        s = _dot_nt(q, k)
        p = jnp.exp(s - jnp.max(s, axis=-1, keepdims=True))
        outs.append(_dot(p, v) / jnp.sum(p, axis=-1, keepdims=True))
    o_ref[...] = (jnp.concatenate(outs, axis=1) * _silu(g_ref[...])).astype(o_ref.dtype)


def _mem_group(q_mem, gate, mem2d, prm, tm):
    s = q_mem.shape[0]
    g = D_GROUP
    n_mem = mem2d.shape[0]
    m = _rmsnorm_rows(mem2d, prm["norm_w"], n_mem)
    kv = _matmul(m, prm["w_kv"].astype(BF16), n_mem, 1024)
    full = lambda shape: pl.BlockSpec(shape, lambda i: (0, 0))
    return pl.pallas_call(
        _mem_attn_kernel,
        grid=(s // tm,),
        in_specs=[pl.BlockSpec((tm, g), lambda i: (i, 0)), pl.BlockSpec((tm, g), lambda i: (i, 3)),
                  full((n_mem, 2 * g)), full((1, MEM_HD)), full((1, MEM_HD))],
        out_specs=pl.BlockSpec((tm, g), lambda i: (i, 0)),
        out_shape=jax.ShapeDtypeStruct((s, g), BF16),
        compiler_params=_cparams("parallel"),
        name="mem_attn",
    )(q_mem, gate, kv, prm["q_norm_w"].reshape(1, -1), prm["k_norm_w"].reshape(1, -1))


def _pad_cols(w, n):
    return jnp.pad(w, ((0, 0), (0, n - w.shape[1])))


def _tile_rows(s, pref):
    t = min(pref, s)
    assert s % t == 0
    return t


def _layer(x, mem2d, lp, seg):
    s = x.shape[0]
    tm = _tile_rows(s, 512)
    w_in = lp["w_in"]
    o_rw = D_MIX
    o_fx = o_rw + RW_COLS
    o_ssm = o_fx + FX_COLS
    o_q = o_ssm + SSM_COLS
    w_gate = w_in[:, :o_rw].astype(BF16)
    w_rw = w_in[:, o_rw:o_fx].astype(BF16)
    w_fx = _pad_cols(w_in[:, o_fx:o_ssm], 3 * D_GROUP + LANES).astype(BF16)
    w_ssm = _pad_cols(w_in[:, o_ssm:o_q], SSM_CONV_DIM + LANES).astype(BF16)
    w_q = w_in[:, o_q:].astype(BF16)

    h = _rmsnorm_rows(x, lp["norm_w"], tm)
    gate = _matmul(h, w_gate, tm, 1024)
    p_rw = _matmul(h, w_rw, tm, 640)
    p_fx = _matmul(h, w_fx, tm, 640)
    p_ssm = _matmul(h, w_ssm, tm, w_ssm.shape[1])
    q_mem = _matmul(h, w_q, tm, 1024)

    y_rw = _rwkv_group(p_rw, gate, lp["rw"], seg, _tile_rows(s, 256))
    y_fx = _fox_group(p_fx, gate, lp["fx"], seg, _tile_rows(s, 256), _tile_rows(s, 512))
    y_ssm = _ssd_group(p_ssm, gate, lp["ssm"], _tile_rows(s, 256))
    y_mem = _mem_group(q_mem, gate, mem2d, lp["mem"], tm)
    return _out_proj(x, (y_rw, y_fx, y_ssm, y_mem), lp["w_out"].astype(BF16), tm, 1024)


def kernel(x, mem, norm_w, w_in, rw_mu, rw_w0, rw_w_up, rw_a0, rw_a_up, rw_k_k, rw_k_a, rw_r_k, rw_lnx_w, rw_lnx_b, fx_f_bias, fx_q_norm_w, fx_k_norm_w, ssm_conv_w, ssm_conv_b, ssm_dt_bias, ssm_a_log, ssm_d, ssm_norm_w, mem_norm_w, mem_w_kv, mem_q_norm_w, mem_k_norm_w, w_out):
    assert x.shape[0] == 1 and mem.shape[0] == 1
    x2 = x[0]
    mem2d = mem[0]
    head_of = jnp.arange(D_GROUP) // HD
    seg = (head_of[:, None] == head_of[None, :]).astype(BF16)
    for l in range(DEPTH):
        lp = {
            "norm_w": norm_w[l], "w_in": w_in[l], "w_out": w_out[l],
            "rw": {"mu": rw_mu[l], "w0": rw_w0[l], "w_up": rw_w_up[l], "a0": rw_a0[l], "a_up": rw_a_up[l],
                   "k_k": rw_k_k[l], "k_a": rw_k_a[l], "r_k": rw_r_k[l], "lnx_w": rw_lnx_w[l], "lnx_b": rw_lnx_b[l]},
            "fx": {"f_bias": fx_f_bias[l], "q_norm_w": fx_q_norm_w[l], "k_norm_w": fx_k_norm_w[l]},
            "ssm": {"conv_w": ssm_conv_w[l], "conv_b": ssm_conv_b[l], "dt_bias": ssm_dt_bias[l], "a_log": ssm_a_log[l],
                    "d": ssm_d[l], "norm_w": ssm_norm_w[l]},
            "mem": {"norm_w": mem_norm_w[l], "w_kv": mem_w_kv[l], "q_norm_w": mem_q_norm_w[l],
                    "k_norm_w": mem_k_norm_w[l]},
        }
        x2 = _layer(x2, mem2d, lp, seg)
    return x2[None]
```

```python
import functools
from typing import NamedTuple

import jax
import jax.numpy as jnp
from jax import lax
from jax.experimental import pallas as pl
from jax.experimental.pallas import tpu as pltpu

F32 = jnp.float32
BF16 = jnp.bfloat16

D_MODEL = 2048
DEPTH = 4
D_MIX = 2 * D_MODEL
D_GROUP = D_MIX // 4
HD = 64
N_HEADS = D_GROUP // HD
RW_LORA = 64
RW_COLS = 3 * D_GROUP + 2 * RW_LORA
RW_GN_EPS = 64e-5
FX_COLS = 3 * D_GROUP + N_HEADS
SSM_GROUPS = 4
SSM_STATE = 128
SSM_CONV = 4
SSM_CONV_DIM = D_GROUP + 2 * SSM_GROUPS * SSM_STATE
SSM_COLS = SSM_CONV_DIM + N_HEADS
MEM_HEADS = 4
MEM_HD = D_GROUP // MEM_HEADS
NORM_EPS = 1e-6
LANES = 128

LOG2E = 1.4426950408889634
FX_AUG = 3
FX_VROWS = HD + 16
RW_CHUNK = 64
RW_PACK = 4
VMEM_LIMIT = 48 * 1024 * 1024

P_GATE = 0
P_RW = P_GATE + D_MIX
P_FX = P_RW + 3 * D_GROUP
P_SSM = P_FX + 3 * D_GROUP
P_QMEM = P_SSM + SSM_CONV_DIM
P_RW_LORA = P_QMEM + D_GROUP
P_FX_F = P_RW_LORA + LANES
P_SSM_DT = P_FX_F + LANES
P_USED = P_SSM_DT + LANES


class _Tiles(NamedTuple):
    proj_m: int
    proj_n: int
    out_n: int
    prep: int
    fox: int
    ssd: int


def _tiles(s):
    pick = lambda pref: min(pref, s)
    t = _Tiles(proj_m=pick(512), proj_n=1536, out_n=1024, prep=pick(256), fox=pick(512), ssd=pick(256))
    assert all(s % v == 0 for v in (t.proj_m, t.prep, t.fox, t.ssd, RW_CHUNK))
    return t


P_WIDTH = -(-P_USED // 1536) * 1536


def _cparams(*sem):
    return pltpu.CompilerParams(dimension_semantics=sem, vmem_limit_bytes=VMEM_LIMIT)


def _dot(a, b):
    return jnp.dot(a.astype(BF16), b.astype(BF16), preferred_element_type=F32)


def _dot_nt(a, b):
    return lax.dot_general(a.astype(BF16), b.astype(BF16), (((1,), (1,)), ((), ())),
                           preferred_element_type=F32)


def _dot_tn(a, b):
    return lax.dot_general(a.astype(BF16), b.astype(BF16), (((0,), (0,)), ((), ())),
                           preferred_element_type=F32)


def _split3(x):
    hi = x.astype(BF16)
    r1 = x - hi.astype(F32)
    mid = r1.astype(BF16)
    lo = (r1 - mid.astype(F32)).astype(BF16)
    return hi, mid, lo


def _dot_exact_lhs(a_bf16, x):
    hi, mid, lo = _split3(x)
    d = lambda p: jnp.dot(a_bf16, p, preferred_element_type=F32)
    return d(hi) + (d(mid) + d(lo))


def _silu(x):
    return x * (1.0 / (1.0 + jnp.exp(-x)))


def _softplus(x):
    return jnp.maximum(x, 0.0) + jnp.log(1.0 + jnp.exp(-jnp.abs(x)))


def _tri_incl(n):
    r = lax.broadcasted_iota(jnp.int32, (n, n), 0)
    c = lax.broadcasted_iota(jnp.int32, (n, n), 1)
    return r >= c


def _expand_heads(s, segt_ref):
    hi = s.astype(BF16)
    lo = (s - hi.astype(F32)).astype(BF16)
    e = segt_ref[...]
    return jnp.dot(hi, e, preferred_element_type=F32) + jnp.dot(lo, e, preferred_element_type=F32)


def _seg_sum(x, seg_ref, segt_ref):
    return _expand_heads(jnp.dot(x.astype(BF16), seg_ref[...], preferred_element_type=F32), segt_ref)


def _rmsnorm_kernel(x_ref, w_ref, o_ref):
    x = x_ref[...]
    ms = jnp.mean(x * x, axis=-1, keepdims=True)
    o_ref[...] = (x * lax.rsqrt(ms + NORM_EPS) * w_ref[...]).astype(o_ref.dtype)


def _rmsnorm_rows(x, w, tm):
    m, d = x.shape
    return pl.pallas_call(
        _rmsnorm_kernel,
        grid=(m // tm,),
        in_specs=[pl.BlockSpec((tm, d), lambda i: (i, 0)), pl.BlockSpec((1, d), lambda i: (0, 0))],
        out_specs=pl.BlockSpec((tm, d), lambda i: (i, 0)),
        out_shape=jax.ShapeDtypeStruct((m, d), BF16),
        compiler_params=_cparams("parallel"),
        name="rmsnorm_rows",
    )(x, w.reshape(1, d))


def _matmul_kernel(a_ref, b_ref, o_ref):
    o_ref[...] = jnp.dot(a_ref[...], b_ref[...], preferred_element_type=F32).astype(o_ref.dtype)


def _matmul(a, b, tm, tn, out_dtype=F32):
    m, k = a.shape
    n = b.shape[1]
    return pl.pallas_call(
        _matmul_kernel,
        grid=(n // tn, m // tm),
        in_specs=[pl.BlockSpec((tm, k), lambda j, i: (i, 0)), pl.BlockSpec((k, tn), lambda j, i: (0, j))],
        out_specs=pl.BlockSpec((tm, tn), lambda j, i: (i, j)),
        out_shape=jax.ShapeDtypeStruct((m, n), out_dtype),
        compiler_params=_cparams("parallel", "parallel"),
        name="matmul",
    )(a, b)


def _out_proj_kernel(x_ref, y0_ref, y1_ref, y2_ref, y3_ref, w_ref, o_ref):
    acc = x_ref[...]
    for g, y_ref in enumerate((y0_ref, y1_ref, y2_ref, y3_ref)):
        acc = acc + jnp.dot(y_ref[...], w_ref[g * D_GROUP:(g + 1) * D_GROUP, :], preferred_element_type=F32)
    o_ref[...] = acc


def _out_proj(x, ys, w_out, tm, tn):
    m, d = x.shape
    yspec = pl.BlockSpec((tm, D_GROUP), lambda j, i: (i, 0))
    return pl.pallas_call(
        _out_proj_kernel,
        grid=(d // tn, m // tm),
        in_specs=[pl.BlockSpec((tm, tn), lambda j, i: (i, j)), yspec, yspec, yspec, yspec,
                  pl.BlockSpec((D_MIX, tn), lambda j, i: (0, j))],
        out_specs=pl.BlockSpec((tm, tn), lambda j, i: (i, j)),
        out_shape=jax.ShapeDtypeStruct((m, d), F32),
        compiler_params=_cparams("parallel", "parallel"),
        name="out_proj",
    )(x, *ys, w_out)


def _pack_cols(w_in):
    o_rw = D_MIX
    o_fx = o_rw + RW_COLS
    o_ssm = o_fx + FX_COLS
    o_q = o_ssm + SSM_COLS
    k = w_in.shape[0]
    pad = lambda w, n: jnp.pad(w, ((0, 0), (0, n - w.shape[1])))
    parts = [w_in[:, :o_rw],
             w_in[:, o_rw:o_rw + 3 * D_GROUP],
             w_in[:, o_fx:o_fx + 3 * D_GROUP],
             w_in[:, o_ssm:o_ssm + SSM_CONV_DIM],
             w_in[:, o_q:],
             w_in[:, o_rw + 3 * D_GROUP:o_fx],
             pad(w_in[:, o_fx + 3 * D_GROUP:o_ssm], LANES),
             pad(w_in[:, o_ssm + SSM_CONV_DIM:o_q], LANES),
             jnp.zeros((k, P_WIDTH - P_USED), w_in.dtype)]
    return jnp.concatenate(parts, axis=1)


def _rw_prep_kernel(r_ref, k_ref, v_ref, lo_ref, pr_ref, pk_ref, pv_ref, plo_ref,
                    mur_ref, muk_ref, muv_ref, mulo_ref, w0_ref, wup_ref, a0_ref, aup_ref,
                    kk_ref, ka_ref, rk_ref, seg_ref, segt_ref,
                    ro_ref, lwo_ref, ko_ref, vo_ref, kko_ref, bo_ref, bonus_ref):
    i = pl.program_id(0)
    t = r_ref.shape[0]
    row = lax.broadcasted_iota(jnp.int32, (t, 1), 0)
    first = i == 0

    def shifted(cur_ref, prev_ref, mu_ref):
        cur = cur_ref[...]
        last = prev_ref[7:8, :]
        last = jnp.where(first, jnp.zeros_like(last), last)
        prev = jnp.where(row == 0, last, pltpu.roll(cur, 1, axis=0))
        return cur + (prev - cur) * mu_ref[...]

    r = shifted(r_ref, pr_ref, mur_ref)
    k = shifted(k_ref, pk_ref, muk_ref)
    v = shifted(v_ref, pv_ref, muv_ref)
    lora = shifted(lo_ref, plo_ref, mulo_ref)

    w = -_softplus(-(w0_ref[...] + _dot(jnp.tanh(lora), wup_ref[...]))) - 0.5
    lw = -jnp.exp(w)
    a = 1.0 / (1.0 + jnp.exp(-(a0_ref[...] + _dot(lora, aup_ref[...]))))
    kk = k * kk_ref[...]
    kk = kk / jnp.maximum(jnp.sqrt(_seg_sum(kk * kk, seg_ref, segt_ref)), 1e-12)
    k2 = k * (1.0 + (a - 1.0) * ka_ref[...])
    ro_ref[...] = r
    lwo_ref[...] = lw
    ko_ref[...] = k2
    vo_ref[...] = v
    kko_ref[...] = kk
    bo_ref[...] = kk * a
    bonus_ref[...] = _seg_sum(r * k2 * rk_ref[...], seg_ref, segt_ref) * v


def _rw_scan_kernel(r_ref, lw_ref, k_ref, v_ref, kk_ref, b_ref, y_ref, state_ref):
    L = r_ref.shape[0]

    @pl.when(pl.program_id(0) == 0)
    def _():
        state_ref[...] = jnp.zeros_like(state_ref)

    lw = lw_ref[...]
    tri = _tri_incl(L)
    c = _dot_exact_lhs(tri.astype(BF16), lw)
    e_pos = jnp.exp(c)
    e_neg = jnp.exp(-c)
    a_hat = -kk_ref[...] * jnp.exp(c - lw)
    b_hat = b_ref[...] * e_neg
    k_hat = k_ref[...] * e_neg
    r_hat = r_ref[...] * e_pos
    v = v_ref[...]
    p_last = e_pos[L - 1:L, :]

    W = RW_PACK * L
    ri = lax.broadcasted_iota(jnp.int32, (W, W), 0)
    ci = lax.broadcasted_iota(jnp.int32, (W, W), 1)
    same = (ri // L) == (ci // L)
    strict = jnp.logical_and(same, (ri & (L - 1)) > (ci & (L - 1)))
    incl = jnp.logical_and(same, (ri & (L - 1)) >= (ci & (L - 1)))
    eye = (ri == ci).astype(F32)
    tile = lambda t: jnp.concatenate([t] * RW_PACK, axis=0)
    own = lambda t: jnp.where(same, tile(t), 0.0)

    packs = range(N_HEADS // RW_PACK)
    each = lambda f, *cols: [f(*(c[g] for c in cols)) for g in packs]
    cat = lambda *ts: jnp.concatenate(ts, axis=0)
    sls = [slice(g * W, (g + 1) * W) for g in packs]
    vw = [own(v[:, sl]).astype(BF16) for sl in sls]
    yb = [tile(b_hat[:, sl]) for sl in sls]
    yk = [tile(k_hat[:, sl]) for sl in sls]
    xar = [cat(own(a_hat[:, sl]), own(r_hat[:, sl])).astype(BF16) for sl in sls]
    ab = each(_dot_nt, xar, yb)
    ak = each(_dot_nt, xar, yk)
    n = each(lambda t: jnp.where(strict, t[:W], 0.0), ab)
    a_k = each(lambda t: cat(jnp.where(strict, t[:W], 0.0), jnp.where(incl, t[W:], 0.0)).astype(BF16), ak)
    a_rb = each(lambda t: jnp.where(incl, t[W:], 0.0).astype(BF16), ab)
    s0 = [state_ref[g] for g in packs]
    xs = each(_dot_nt, xar, s0)
    av = each(_dot, a_k, vw)
    tinv = each(lambda t: eye + t, n)
    p = each(_dot, n, n)
    for _ in range(4):
        pt = each(lambda pp, tt: _dot(cat(pp, tt), pp), p, tinv)
        p = each(lambda t: t[:W], pt)
        tinv = each(lambda tt, t: tt + t[W:], tinv, pt)
    tinv = each(lambda tt, pp: tt + _dot(tt, pp), tinv, p)
    u = each(lambda tt, x, a: _dot(tt, x[:W] + a[:W]).astype(BF16), tinv, xs, av)
    yw = each(lambda x, a, arb, uu: x[W:] + a[W:] + _dot(arb, uu), xs, av, a_rb, u)
    upd = each(lambda uu, vv, b, k: _dot_tn(cat(uu, vv), cat(jnp.where(same, b, 0.0), jnp.where(same, k, 0.0))),
               u, vw, yb, yk)
    for g in packs:
        y_ref[:, sls[g]] = functools.reduce(jnp.add, [yw[g][i * L:(i + 1) * L] for i in range(RW_PACK)])
        state_ref[g] = (s0[g] + upd[g]) * p_last[:, sls[g]]


def _rw_post_kernel(y_ref, bonus_ref, g_ref, lw_ref, lb_ref, seg_ref, segt_ref, o_ref):
    y = y_ref[...]
    mean = _seg_sum(y, seg_ref, segt_ref) * (1.0 / HD)
    d = y - mean
    var = _seg_sum(d * d, seg_ref, segt_ref) * (1.0 / HD)
    yn = d * lax.rsqrt(var + RW_GN_EPS) * lw_ref[...] + lb_ref[...] + bonus_ref[...]
    o_ref[...] = (yn * _silu(g_ref[...])).astype(o_ref.dtype)


def _rwkv_group(proj, prm, seg, segt, tiles):
    s = proj.shape[0]
    g = D_GROUP
    tm = tiles.prep
    row = lambda v: v.reshape(1, -1)
    wup = jnp.concatenate([prm["w_up"], jnp.zeros_like(prm["w_up"])], axis=0).astype(BF16)
    aup = jnp.concatenate([jnp.zeros_like(prm["a_up"]), prm["a_up"]], axis=0).astype(BF16)
    mu = prm["mu"]
    full = lambda shape: pl.BlockSpec(shape, lambda i: (0, 0))
    cur = lambda off, w: pl.BlockSpec((tm, w), lambda i: (i, off // w))
    prev = lambda off, w: pl.BlockSpec((8, w), lambda i: (jnp.maximum(i * (tm // 8) - 1, 0), off // w))
    fields = [(P_RW, g), (P_RW + g, g), (P_RW + 2 * g, g), (P_RW_LORA, LANES)]
    outs = pl.pallas_call(
        _rw_prep_kernel,
        grid=(s // tm,),
        in_specs=[cur(*f) for f in fields] + [prev(*f) for f in fields]
        + [full((1, g))] * 3 + [full((1, LANES)), full((1, g)), full((LANES, g)), full((1, g)), full((LANES, g)),
                                full((1, g)), full((1, g)), full((1, g)), full((g, LANES)), full((LANES, g))],
        out_specs=[pl.BlockSpec((tm, g), lambda i: (i, 0))] * 7,
        out_shape=[jax.ShapeDtypeStruct((s, g), F32)] * 7,
        compiler_params=_cparams("parallel"),
        name="rwkv_prep",
    )(*([proj] * 8), row(mu[:g]), row(mu[g:2 * g]), row(mu[2 * g:3 * g]), row(mu[3 * g:]),
      row(prm["w0"]), wup, row(prm["a0"]), aup, row(prm["k_k"]), row(prm["k_a"]), row(prm["r_k"]), seg, segt)
    r, lw, k2, v, kk, b, bonus = outs
    L = RW_CHUNK
    blk = pl.BlockSpec((L, g), lambda i: (i, 0))
    y = pl.pallas_call(
        _rw_scan_kernel,
        grid=(s // L,),
        in_specs=[blk] * 6,
        out_specs=blk,
        out_shape=jax.ShapeDtypeStruct((s, g), F32),
        scratch_shapes=[pltpu.VMEM((N_HEADS // RW_PACK, RW_PACK * HD, RW_PACK * HD), F32)],
        compiler_params=_cparams("arbitrary"),
        name="rwkv_scan",
    )(r, lw, k2, v, kk, b)
    rowblk = pl.BlockSpec((tm, g), lambda i: (i, 0))
    return pl.pallas_call(
        _rw_post_kernel,
        grid=(s // tm,),
        in_specs=[rowblk, rowblk, pl.BlockSpec((tm, g), lambda i: (i, P_GATE // g)), full((1, g)), full((1, g)),
                  full((g, LANES)), full((LANES, g))],
        out_specs=rowblk,
        out_shape=jax.ShapeDtypeStruct((s, g), BF16),
        compiler_params=_cparams("parallel"),
        name="rwkv_post",
    )(y, bonus, proj, row(prm["lnx_w"]), row(prm["lnx_b"]), seg, segt)


def _fx_prep_kernel(q_ref, k_ref, v_ref, f_ref, qw_ref, kw_ref, fb_ref, seg_ref, segt_ref,
                    qo_ref, ko_ref, vo_ref, carry_ref):
    t = q_ref.shape[0]

    @pl.when(pl.program_id(0) == 0)
    def _():
        carry_ref[...] = jnp.zeros_like(carry_ref)

    def headnorm(x, w):
        ms = _seg_sum(x * x, seg_ref, segt_ref) * (1.0 / HD)
        return x * lax.rsqrt(ms + NORM_EPS) * w

    qn = headnorm(q_ref[...], qw_ref[...]) * (HD ** -0.5 * LOG2E)
    kn = headnorm(k_ref[...], kw_ref[...])
    z = f_ref[...] + fb_ref[...]
    log_f = jnp.minimum(z, 0.0) - jnp.log(1.0 + jnp.exp(-jnp.abs(z)))
    c = _dot_exact_lhs(_tri_incl(t).astype(BF16), log_f) + carry_ref[...]
    carry_ref[...] = c[t - 1:t, :]
    vt = v_ref[...].T.astype(vo_ref.dtype)
    lane = lax.broadcasted_iota(jnp.int32, (t, LANES), 1)
    for h in range(N_HEADS):
        pair = slice((h // 2) * LANES, (h // 2 + 1) * LANES)
        own = (lane >= HD) if h % 2 else (lane < HD)
        aug0 = 0 if h % 2 else HD
        hi, mid, lo = (piece.astype(F32) for piece in _split3(jnp.broadcast_to(c[:, h:h + 1] * (-LOG2E), (t, LANES))))
        k_aug = jnp.where(lane == aug0, hi, jnp.where(lane == aug0 + 1, mid, jnp.where(lane == aug0 + 2, lo, 0.0)))
        q_aug = jnp.where(lane < aug0 + FX_AUG, 1.0, 0.0)
        ko_ref[h] = jnp.where(own, kn[:, pair], k_aug).astype(BF16)
        qo_ref[h] = jnp.where(own, qn[:, pair], q_aug).astype(BF16)
        vo_ref[h, :HD, :] = vt[h * HD:(h + 1) * HD, :]
        vo_ref[h, HD:, :] = jnp.ones((FX_VROWS - HD, t), vo_ref.dtype)


def _fx_attn_kernel(q_ref, k_ref, vt_ref, g_ref, o_ref, s_ref, *, tk):
    i = pl.program_id(1)
    tq = q_ref.shape[1]
    heads = range(2)
    q = [q_ref[e] for e in heads]
    rel = lax.broadcasted_iota(jnp.int32, (tk, tq), 0) - lax.broadcasted_iota(jnp.int32, (tk, tq), 1)

    def rows(j):
        return pl.ds(pl.multiple_of(j * tk, tk), tk)

    def scores(j, buf):
        for e in heads:
            s_ref[buf, e] = lax.dot_general(k_ref[e, rows(j), :], q[e], (((1,), (1,)), ((), ())),
                                            preferred_element_type=F32)

    def absorb(j, buf, carry, masked):
        out = []
        for e in heads:
            s = s_ref[buf, e]
            if masked:
                s = jnp.where(rel <= i * tq - j * tk, s, -1e30)
            m, acc = carry[e]
            m_new = jnp.maximum(m, jnp.max(s, axis=0, keepdims=True))
            p = jnp.exp2(s - m_new).astype(BF16)
            pv = jnp.dot(vt_ref[e, :, rows(j)], p, preferred_element_type=F32)
            out.append((m_new, jnp.exp2(m - m_new) * acc + pv))
        return tuple(out)

    def pair(jj, carry):
        scores(2 * jj + 1, 1)
        carry = absorb(2 * jj, 0, carry, False)
        scores(2 * jj + 2, 0)
        return absorb(2 * jj + 1, 1, carry, False)

    def finish(carry):
        ot = jnp.concatenate([carry[e][1][:HD] / carry[e][1][HD:HD + 1] for e in heads], axis=0)
        o_ref[...] = (ot.T * _silu(g_ref[...])).astype(o_ref.dtype)

    init = tuple((jnp.full((1, tq), -1e30, F32), jnp.zeros((FX_VROWS, tq), F32)) for _ in heads)
    scores(0, 0)
    carry = lax.fori_loop(0, i // 2, pair, init)

    @pl.when(i % 2 == 0)
    def _():
        finish(absorb(i, 0, carry, True))

    @pl.when(i % 2 == 1)
    def _():
        scores(i, 1)
        finish(absorb(i, 1, absorb(i - 1, 0, carry, False), True))


def _fox_group(proj, prm, seg, segt, tiles):
    s = proj.shape[0]
    g = D_GROUP
    tm, tq = tiles.prep, tiles.fox
    full = lambda shape: pl.BlockSpec(shape, lambda i: (0, 0))
    col = lambda off, w: pl.BlockSpec((tm, w), lambda i: (i, off // w))
    tile_w = lambda w: jnp.tile(w, N_HEADS).reshape(1, g)
    fb = jnp.zeros((1, LANES), F32).at[0, :N_HEADS].set(prm["f_bias"])
    q_aug, k_aug, vt_aug = pl.pallas_call(
        _fx_prep_kernel,
        grid=(s // tm,),
        in_specs=[col(P_FX, g), col(P_FX + g, g), col(P_FX + 2 * g, g), col(P_FX_F, LANES),
                  full((1, g)), full((1, g)), full((1, LANES)), full((g, LANES)), full((LANES, g))],
        out_specs=[pl.BlockSpec((N_HEADS, tm, LANES), lambda i: (0, i, 0)),
                   pl.BlockSpec((N_HEADS, tm, LANES), lambda i: (0, i, 0)),
                   pl.BlockSpec((N_HEADS, FX_VROWS, tm), lambda i: (0, 0, i))],
        out_shape=[jax.ShapeDtypeStruct((N_HEADS, s, LANES), BF16)] * 2
        + [jax.ShapeDtypeStruct((N_HEADS, FX_VROWS, s), BF16)],
        scratch_shapes=[pltpu.VMEM((1, LANES), F32)],
        compiler_params=_cparams("arbitrary"),
        name="fox_prep",
    )(proj, proj, proj, proj, tile_w(prm["q_norm_w"]), tile_w(prm["k_norm_w"]), fb, seg, segt)
    return pl.pallas_call(
        functools.partial(_fx_attn_kernel, tk=tq),
        grid=(N_HEADS // 2, s // tq),
        in_specs=[pl.BlockSpec((2, tq, LANES), lambda j, i: (j, i, 0)),
                  pl.BlockSpec((2, s, LANES), lambda j, i: (j, 0, 0)),
                  pl.BlockSpec((2, FX_VROWS, s), lambda j, i: (j, 0, 0)),
                  pl.BlockSpec((tq, LANES), lambda j, i: (i, (P_GATE + g) // LANES + j))],
        out_specs=pl.BlockSpec((tq, LANES), lambda j, i: (i, j)),
        out_shape=jax.ShapeDtypeStruct((s, g), BF16),
        scratch_shapes=[pltpu.VMEM((2, 2, tq, tq), F32)],
        compiler_params=_cparams("parallel", "parallel"),
        name="fox_attn",
    )(q_aug, k_aug, vt_aug, proj)


def _ssd_kernel(xbc_ref, dt_ref, z_ref, cw_ref, cb_ref, dtb_ref, alog_ref, dskip_ref, nw_ref, segt_ref,
                o_ref, tail_ref, state_ref):
    t = xbc_ref.shape[0]
    g = D_GROUP
    n = SSM_STATE

    @pl.when(pl.program_id(0) == 0)
    def _():
        tail_ref[...] = jnp.zeros_like(tail_ref)
        state_ref[...] = jnp.zeros_like(state_ref)

    cur = xbc_ref[...]
    tail = tail_ref[...]
    row8 = lax.broadcasted_iota(jnp.int32, (8, 1), 0)
    conv = cur * cw_ref[SSM_CONV - 1:SSM_CONV, :] + cb_ref[...]
    for sft in range(1, SSM_CONV):
        rolled = pltpu.roll(cur, sft, axis=0)
        head = jnp.where(row8 < sft, pltpu.roll(tail, sft, axis=0), rolled[:8])
        shifted = jnp.concatenate([head, rolled[8:]], axis=0)
        conv = conv + shifted * cw_ref[SSM_CONV - 1 - sft:SSM_CONV - sft, :]
    tail_ref[...] = cur[t - 8:, :]
    xbc = _silu(conv)
    xs = xbc[:, :g]
    dt = _softplus(dt_ref[...] + dtb_ref[...])
    da = dt * (-jnp.exp(alog_ref[...]))
    tri = _tri_incl(t)
    acs = _dot_exact_lhs(tri.astype(BF16), da)
    acs_t = acs.T
    e_acs = jnp.exp(acs)
    tile_decay = e_acs[t - 1:t, :]
    xdt = xs * _expand_heads(dt, segt_ref)
    xdec = xdt * _expand_heads(jnp.exp(acs[t - 1:t, :] - acs), segt_ref)
    y_diag, y_off = [], []
    for grp in range(SSM_GROUPS):
        bm = xbc[:, g + grp * n:g + (grp + 1) * n]
        cm = xbc[:, g + SSM_GROUPS * n + grp * n:g + SSM_GROUPS * n + (grp + 1) * n]
        cb = _dot_nt(cm, bm)
        for hh in range(N_HEADS // SSM_GROUPS):
            h = grp * (N_HEADS // SSM_GROUPS) + hh
            hs = slice(h * HD, (h + 1) * HD)
            lmat = jnp.exp(jnp.where(tri, acs[:, h:h + 1] - acs_t[h:h + 1, :], -jnp.inf))
            st = state_ref[h]
            y_diag.append(_dot(cb * lmat, xdt[:, hs]))
            y_off.append(_dot_nt(cm, st))
            state_ref[h] = st * tile_decay[:, h:h + 1] + _dot_tn(xdec[:, hs], bm)
    y = jnp.concatenate(y_diag, axis=1) + jnp.concatenate(y_off, axis=1) * _expand_heads(e_acs, segt_ref)
    y = (y + dskip_ref[...] * xs) * _silu(z_ref[...])
    gw = g // SSM_GROUPS
    outs = []
    for grp in range(SSM_GROUPS):
        yg = y[:, grp * gw:(grp + 1) * gw]
        outs.append(yg * lax.rsqrt(jnp.mean(yg * yg, axis=-1, keepdims=True) + NORM_EPS))
    o_ref[...] = (jnp.concatenate(outs, axis=1) * nw_ref[...]).astype(o_ref.dtype)


def _ssd_group(proj, prm, segt, tiles):
    s = proj.shape[0]
    g = D_GROUP
    t = tiles.ssd
    full = lambda shape: pl.BlockSpec(shape, lambda i: (0, 0))
    pad_heads = lambda v: jnp.zeros((1, LANES), F32).at[0, :N_HEADS].set(v)
    return pl.pallas_call(
        _ssd_kernel,
        grid=(s // t,),
        in_specs=[pl.BlockSpec((t, SSM_CONV_DIM), lambda i: (i, P_SSM // SSM_CONV_DIM)),
                  pl.BlockSpec((t, LANES), lambda i: (i, P_SSM_DT // LANES)),
                  pl.BlockSpec((t, g), lambda i: (i, (P_GATE + 2 * g) // g)),
                  full((SSM_CONV, SSM_CONV_DIM)), full((1, SSM_CONV_DIM)), full((1, LANES)), full((1, LANES)),
                  full((1, g)), full((1, g)), full((LANES, g))],
        out_specs=pl.BlockSpec((t, g), lambda i: (i, 0)),
        out_shape=jax.ShapeDtypeStruct((s, g), BF16),
        scratch_shapes=[pltpu.VMEM((8, SSM_CONV_DIM), F32), pltpu.VMEM((N_HEADS, HD, SSM_STATE), F32)],
        compiler_params=_cparams("arbitrary"),
        name="ssd_scan",
    )(proj, proj, proj, prm["conv_w"], prm["conv_b"].reshape(1, -1), pad_heads(prm["dt_bias"]),
      pad_heads(prm["a_log"]), jnp.repeat(prm["d"], HD).reshape(1, g), prm["norm_w"].reshape(1, -1), segt)


def _mem_attn_kernel(q_ref, g_ref, kv_ref, qw_ref, kw_ref, o_ref):
    outs = []
    for h in range(MEM_HEADS):
        sl = slice(h * MEM_HD, (h + 1) * MEM_HD)

        def norm(x, w):
            return x * lax.rsqrt(jnp.mean(x * x, axis=-1, keepdims=True) + NORM_EPS) * w

        q = norm(q_ref[:, sl], qw_ref[...]) * (MEM_HD ** -0.5)
        k = norm(kv_ref[:, sl], kw_ref[...])
        v = kv_ref[:, D_GROUP + h * MEM_HD:D_GROUP + (h + 1) * MEM_HD]
        s = _dot_nt(q, k)
        p = jnp.exp(s - jnp.max(s, axis=-1, keepdims=True))
        outs.append(_dot(p, v) / jnp.sum(p, axis=-1, keepdims=True))
    o_ref[...] = (jnp.concatenate(outs, axis=1) * _silu(g_ref[...])).astype(o_ref.dtype)


def _mem_group(proj, mem2d, prm, tiles):
    s = proj.shape[0]
    g = D_GROUP
    tm = tiles.proj_m
    n_mem = mem2d.shape[0]
    m = _rmsnorm_rows(mem2d, prm["norm_w"], n_mem)
    kv = _matmul(m, prm["w_kv"].astype(BF16), n_mem, tiles.out_n)
    full = lambda shape: pl.BlockSpec(shape, lambda i: (0, 0))
    return pl.pallas_call(
        _mem_attn_kernel,
        grid=(s // tm,),
        in_specs=[pl.BlockSpec((tm, g), lambda i: (i, P_QMEM // g)),
                  pl.BlockSpec((tm, g), lambda i: (i, (P_GATE + 3 * g) // g)),
                  full((n_mem, 2 * g)), full((1, MEM_HD)), full((1, MEM_HD))],
        out_specs=pl.BlockSpec((tm, g), lambda i: (i, 0)),
        out_shape=jax.ShapeDtypeStruct((s, g), BF16),
        compiler_params=_cparams("parallel"),
        name="mem_attn",
    )(proj, proj, kv, prm["q_norm_w"].reshape(1, -1), prm["k_norm_w"].reshape(1, -1))


def _layer(x, mem2d, lp, seg, segt):
    tiles = _tiles(x.shape[0])
    h = _rmsnorm_rows(x, lp["norm_w"], tiles.proj_m)
    proj = _matmul(h, _pack_cols(lp["w_in"]).astype(BF16), tiles.proj_m, tiles.proj_n)
    y_rw = _rwkv_group(proj, lp["rw"], seg, segt, tiles)
    y_fx = _fox_group(proj, lp["fx"], seg, segt, tiles)
    y_ssm = _ssd_group(proj, lp["ssm"], segt, tiles)
    y_mem = _mem_group(proj, mem2d, lp["mem"], tiles)
    return _out_proj(x, (y_rw, y_fx, y_ssm, y_mem), lp["w_out"].astype(BF16), tiles.proj_m, tiles.out_n)


def kernel(x, mem, norm_w, w_in, rw_mu, rw_w0, rw_w_up, rw_a0, rw_a_up, rw_k_k, rw_k_a, rw_r_k, rw_lnx_w, rw_lnx_b, fx_f_bias, fx_q_norm_w, fx_k_norm_w, ssm_conv_w, ssm_conv_b, ssm_dt_bias, ssm_a_log, ssm_d, ssm_norm_w, mem_norm_w, mem_w_kv, mem_q_norm_w, mem_k_norm_w, w_out):
    assert x.shape[0] == 1 and mem.shape[0] == 1
    x2 = x[0]
    mem2d = mem[0]
    seg = (jnp.arange(D_GROUP)[:, None] // HD == jnp.arange(LANES)[None, :]).astype(BF16)
    segt = seg.T
    for l in range(DEPTH):
        lp = {
            "norm_w": norm_w[l], "w_in": w_in[l], "w_out": w_out[l],
            "rw": {"mu": rw_mu[l], "w0": rw_w0[l], "w_up": rw_w_up[l], "a0": rw_a0[l], "a_up": rw_a_up[l],
                   "k_k": rw_k_k[l], "k_a": rw_k_a[l], "r_k": rw_r_k[l], "lnx_w": rw_lnx_w[l], "lnx_b": rw_lnx_b[l]},
            "fx": {"f_bias": fx_f_bias[l], "q_norm_w": fx_q_norm_w[l], "k_norm_w": fx_k_norm_w[l]},
            "ssm": {"conv_w": ssm_conv_w[l], "conv_b": ssm_conv_b[l], "dt_bias": ssm_dt_bias[l], "a_log": ssm_a_log[l],
                    "d": ssm_d[l], "norm_w": ssm_norm_w[l]},
            "mem": {"norm_w": mem_norm_w[l], "w_kv": mem_w_kv[l], "q_norm_w": mem_q_norm_w[l],
                    "k_norm_w": mem_k_norm_w[l]},
        }
        x2 = _layer(x2, mem2d, lp, seg, segt)
    return x2[None]
```

```python
import functools
from typing import NamedTuple

import jax
import jax.numpy as jnp
from jax import lax
from jax.experimental import pallas as pl
from jax.experimental.pallas import tpu as pltpu

F32 = jnp.float32
BF16 = jnp.bfloat16

D_MODEL = 2048
DEPTH = 4
D_MIX = 2 * D_MODEL
D_GROUP = D_MIX // 4
HD = 64
N_HEADS = D_GROUP // HD
RW_LORA = 64
RW_COLS = 3 * D_GROUP + 2 * RW_LORA
RW_GN_EPS = 64e-5
RW_DECAY_SCALE = 0.6065306597126334
FX_COLS = 3 * D_GROUP + N_HEADS
SSM_GROUPS = 4
SSM_STATE = 128
SSM_CONV = 4
SSM_CONV_DIM = D_GROUP + 2 * SSM_GROUPS * SSM_STATE
SSM_COLS = SSM_CONV_DIM + N_HEADS
MEM_HEADS = 4
MEM_HD = D_GROUP // MEM_HEADS
NORM_EPS = 1e-6
LANES = 128

LOG2E = 1.4426950408889634
FX_AUG = 3
FX_HEADS_PER_STEP = 4
FX_VROWS = HD + 16
RW_CHUNK = 64
RW_PACK = 4
VMEM_LIMIT = 48 * 1024 * 1024

P_GATE = 0
P_RW = P_GATE + D_MIX
P_FX = P_RW + 3 * D_GROUP
P_SSM = P_FX + 3 * D_GROUP
P_QMEM = P_SSM + SSM_CONV_DIM
P_RW_LORA = P_QMEM + D_GROUP
P_FX_F = P_RW_LORA + LANES
P_SSM_DT = P_FX_F + LANES
P_USED = P_SSM_DT + LANES


class _Tiles(NamedTuple):
    proj_m: int
    proj_n: int
    out_n: int
    kv_n: int
    pack_rows: int
    rw_prep: int
    prep: int
    fox: int
    ssd: int


PROJ_N = 2304
PACK_CHUNK = 1024


def _tiles(s):
    pick = lambda pref: min(pref, s)
    t = _Tiles(proj_m=pick(512), proj_n=PROJ_N, out_n=1024, kv_n=1024, pack_rows=128, rw_prep=pick(512), prep=pick(256), fox=pick(512), ssd=pick(256))
    assert all(s % v == 0 for v in (t.proj_m, t.rw_prep, t.prep, t.fox, t.ssd, RW_CHUNK))
    return t


P_WIDTH = -(-P_USED // PROJ_N) * PROJ_N


def _cparams(*sem, flags=None):
    return pltpu.CompilerParams(dimension_semantics=sem, vmem_limit_bytes=VMEM_LIMIT, flags=flags)


def _dot(a, b):
    return jnp.dot(a.astype(BF16), b.astype(BF16), preferred_element_type=F32)


def _dot_nt(a, b):
    return lax.dot_general(a.astype(BF16), b.astype(BF16), (((1,), (1,)), ((), ())),
                           preferred_element_type=F32)


def _dot_tn(a, b):
    return lax.dot_general(a.astype(BF16), b.astype(BF16), (((0,), (0,)), ((), ())),
                           preferred_element_type=F32)


def _split3(x):
    hi = x.astype(BF16)
    r1 = x - hi.astype(F32)
    mid = r1.astype(BF16)
    lo = (r1 - mid.astype(F32)).astype(BF16)
    return hi, mid, lo


def _dot_exact_lhs(a_bf16, x):
    hi, mid, lo = _split3(x)
    d = lambda p: jnp.dot(a_bf16, p, preferred_element_type=F32)
    return d(hi) + (d(mid) + d(lo))


def _silu(x):
    return x * (1.0 / (1.0 + jnp.exp(-x)))


def _softplus(x):
    return jnp.maximum(x, 0.0) + jnp.log(1.0 + jnp.exp(-jnp.abs(x)))


def _tri_incl(n):
    r = lax.broadcasted_iota(jnp.int32, (n, n), 0)
    c = lax.broadcasted_iota(jnp.int32, (n, n), 1)
    return r >= c


def _expand_heads(s, segt_ref):
    hi = s.astype(BF16)
    lo = (s - hi.astype(F32)).astype(BF16)
    e = segt_ref[...]
    return jnp.dot(hi, e, preferred_element_type=F32) + jnp.dot(lo, e, preferred_element_type=F32)


def _seg_sum(x, seg_ref, segt_ref):
    return _expand_heads(jnp.dot(x.astype(BF16), seg_ref[...], preferred_element_type=F32), segt_ref)


def _norm_matmul_kernel(x_ref, nw_ref, b_ref, o_ref):
    x = x_ref[...]
    ms = jnp.mean(x * x, axis=-1, keepdims=True)
    h = (x * lax.rsqrt(ms + NORM_EPS) * nw_ref[...]).astype(BF16)
    o_ref[...] = jnp.dot(h, b_ref[...].astype(BF16), preferred_element_type=F32)


def _layer_block(shape, layer, index_map):
    if layer is None:
        return pl.BlockSpec(shape, index_map)
    return pl.BlockSpec((None,) + shape, lambda *g: (layer,) + index_map(*g))


def _norm_matmul(x, norm_w, b, tm, tn, layer=None):
    m, k = x.shape
    n = b.shape[-1]
    nw = norm_w.reshape(norm_w.shape[:-1] + (1, k))
    return pl.pallas_call(
        _norm_matmul_kernel,
        grid=(n // tn, m // tm),
        in_specs=[pl.BlockSpec((tm, k), lambda j, i: (i, 0)), _layer_block((1, k), layer, lambda j, i: (0, 0)),
                  _layer_block((k, tn), layer, lambda j, i: (0, j))],
        out_specs=pl.BlockSpec((tm, tn), lambda j, i: (i, j)),
        out_shape=jax.ShapeDtypeStruct((m, n), F32),
        compiler_params=_cparams("parallel", "parallel"),
        name="norm_matmul",
    )(x, nw, b)


def _out_proj_kernel(x_ref, y0_ref, y1_ref, y2_ref, y3_ref, w_ref, o_ref, wb_ref):
    @pl.when(pl.program_id(1) == 0)
    def _():
        wb_ref[...] = w_ref[...].astype(BF16)

    acc = x_ref[...]
    for g, y_ref in enumerate((y0_ref, y1_ref, y2_ref, y3_ref)):
        acc = acc + jnp.dot(y_ref[...], wb_ref[g * D_GROUP:(g + 1) * D_GROUP, :], preferred_element_type=F32)
    o_ref[...] = acc


def _out_proj(x, ys, w_out, layer, tm, tn):
    m, d = x.shape
    yspec = pl.BlockSpec((tm, D_GROUP), lambda j, i: (i, 0))
    return pl.pallas_call(
        _out_proj_kernel,
        grid=(d // tn, m // tm),
        in_specs=[pl.BlockSpec((tm, tn), lambda j, i: (i, j)), yspec, yspec, yspec, yspec,
                  pl.BlockSpec((None, D_MIX, tn), lambda j, i: (layer, 0, j), pipeline_mode=pl.Buffered(1))],
        out_specs=pl.BlockSpec((tm, tn), lambda j, i: (i, j)),
        out_shape=jax.ShapeDtypeStruct((m, d), F32),
        scratch_shapes=[pltpu.VMEM((D_MIX, tn), BF16)],
        compiler_params=_cparams("parallel", "arbitrary"),
        name="out_proj",
    )(x, *ys, w_out)


def _pack_cols(w_in):
    o_rw = D_MIX
    o_fx = o_rw + RW_COLS
    o_ssm = o_fx + FX_COLS
    o_q = o_ssm + SSM_COLS
    k = w_in.shape[0]
    pad = lambda w, n: jnp.pad(w, ((0, 0), (0, n - w.shape[1])))
    parts = [w_in[:, :o_rw],
             w_in[:, o_rw:o_rw + 3 * D_GROUP],
             w_in[:, o_fx:o_fx + 3 * D_GROUP],
             w_in[:, o_ssm:o_ssm + SSM_CONV_DIM],
             w_in[:, o_q:],
             w_in[:, o_rw + 3 * D_GROUP:o_fx],
             pad(w_in[:, o_fx + 3 * D_GROUP:o_ssm], LANES),
             pad(w_in[:, o_ssm + SSM_CONV_DIM:o_q], LANES),
             jnp.zeros((k, P_WIDTH - P_USED), w_in.dtype)]
    return jnp.concatenate(parts, axis=1)


def _pack_w_kernel(wt_ref, o_ref):
    kb = wt_ref.shape[1]
    o_rw = D_MIX
    o_fx = o_rw + RW_COLS
    o_ssm = o_fx + FX_COLS
    o_q = o_ssm + SSM_COLS

    def move(dst, src, width):
        for off in range(0, width, PACK_CHUNK):
            w = min(PACK_CHUNK, width - off)
            o_ref[:, dst + off:dst + off + w] = wt_ref[src + off:src + off + w, :].T.astype(BF16)

    def move_heads(dst, src):
        row = lax.broadcasted_iota(jnp.int32, (LANES, kb), 0)
        o_ref[:, dst:dst + LANES] = jnp.where(row < N_HEADS, wt_ref[src:src + LANES, :], 0.0).T.astype(BF16)

    move(P_GATE, 0, D_MIX)
    move(P_RW, o_rw, 3 * D_GROUP)
    move(P_FX, o_fx, 3 * D_GROUP)
    move(P_SSM, o_ssm, SSM_CONV_DIM)
    move(P_QMEM, o_q, D_GROUP)
    move(P_RW_LORA, o_rw + 3 * D_GROUP, 2 * RW_LORA)
    move_heads(P_FX_F, o_fx + 3 * D_GROUP)
    move_heads(P_SSM_DT, o_ssm + SSM_CONV_DIM)
    o_ref[:, P_USED:] = jnp.zeros((kb, P_WIDTH - P_USED), BF16)


def _pack_w_in(w_in_t_all, layer, kb):
    depth, n, k = w_in_t_all.shape
    return pl.pallas_call(
        _pack_w_kernel,
        grid=(k // kb,),
        in_specs=[pl.BlockSpec((None, n, kb), lambda i: (layer, 0, i))],
        out_specs=pl.BlockSpec((kb, P_WIDTH), lambda i: (i, 0)),
        out_shape=jax.ShapeDtypeStruct((k, P_WIDTH), BF16),
        compiler_params=_cparams("parallel"),
        name="pack_w_in",
    )(w_in_t_all)


def _rw_prep_kernel(r_ref, k_ref, v_ref, lo_ref, pr_ref, pk_ref, pv_ref, plo_ref,
                    mur_ref, muk_ref, muv_ref, mulo_ref, w0_ref, wup_ref, a0_ref, aup_ref,
                    kk_ref, ka_ref, rk_ref, seg_ref, segt_ref,
                    ro_ref, lwo_ref, ko_ref, vo_ref, kko_ref, bo_ref, bonus_ref):
    i = pl.program_id(0)
    t = r_ref.shape[0]
    row = lax.broadcasted_iota(jnp.int32, (t, 1), 0)
    first = i == 0

    def shifted(cur_ref, prev_ref, mu_ref):
        cur = cur_ref[...]
        last = prev_ref[7:8, :]
        last = jnp.where(first, jnp.zeros_like(last), last)
        prev = jnp.where(row == 0, last, pltpu.roll(cur, 1, axis=0))
        return cur + (prev - cur) * mu_ref[...]

    r = shifted(r_ref, pr_ref, mur_ref)
    k = shifted(k_ref, pk_ref, muk_ref)
    v = shifted(v_ref, pv_ref, muv_ref)
    lora = shifted(lo_ref, plo_ref, mulo_ref)

    z = w0_ref[...] + _dot(jnp.tanh(lora), wup_ref[...])
    lw = -RW_DECAY_SCALE / (1.0 + jnp.exp(-z))
    a = 1.0 / (1.0 + jnp.exp(-(a0_ref[...] + _dot(lora, aup_ref[...]))))
    kk = k * kk_ref[...]
    kk = kk * lax.rsqrt(jnp.maximum(_seg_sum(kk * kk, seg_ref, segt_ref), 1e-24))
    k2 = k * (1.0 + (a - 1.0) * ka_ref[...])
    ro_ref[...] = r.astype(ro_ref.dtype)
    lwo_ref[...] = lw
    ko_ref[...] = k2.astype(ko_ref.dtype)
    vo_ref[...] = v.astype(vo_ref.dtype)
    kko_ref[...] = kk.astype(kko_ref.dtype)
    bo_ref[...] = (kk * a).astype(bo_ref.dtype)
    bonus_ref[...] = (_seg_sum(r * k2 * rk_ref[...], seg_ref, segt_ref) * v).astype(bonus_ref.dtype)


def _rw_scan_kernel(r_ref, lw_ref, k_ref, v_ref, kk_ref, b_ref, bonus_ref, g_ref, lnw_ref, lnb_ref, y_ref, state_ref):
    L = r_ref.shape[0]

    @pl.when(pl.program_id(0) == 0)
    def _():
        state_ref[...] = jnp.zeros_like(state_ref)

    tri = _tri_incl(L).astype(BF16)

    W = RW_PACK * L
    ri = lax.broadcasted_iota(jnp.int32, (W, W), 0)
    ci = lax.broadcasted_iota(jnp.int32, (W, W), 1)
    same = (ri // L) == (ci // L)
    strict = jnp.logical_and(same, (ri & (L - 1)) > (ci & (L - 1)))
    incl = jnp.logical_and(same, (ri & (L - 1)) >= (ci & (L - 1)))
    eye = (ri == ci).astype(F32)
    tile = lambda t: jnp.concatenate([t] * RW_PACK, axis=0)
    own = lambda t: jnp.where(same, tile(t), 0.0)

    packs = range(N_HEADS // RW_PACK)
    each = lambda f, *cols: [f(*(c[g] for c in cols)) for g in packs]
    cat = lambda *ts: jnp.concatenate(ts, axis=0)
    sls = [slice(g * W, (g + 1) * W) for g in packs]
    lw = [lw_ref[:, sl] for sl in sls]
    c = [_dot_exact_lhs(tri, t) for t in lw]
    e_pos = each(jnp.exp, c)
    e_neg = each(lambda t: jnp.exp(-t), c)
    p_last = each(lambda t: t[L - 1:L, :], e_pos)
    vw = [own(v_ref[:, sl]).astype(BF16) for sl in sls]
    yb = [tile(b_ref[:, sl] * e_neg[g]) for g, sl in enumerate(sls)]
    yk = [tile(k_ref[:, sl] * e_neg[g]) for g, sl in enumerate(sls)]
    xar = [cat(own(-kk_ref[:, sl] * jnp.exp(c[g] - lw[g])), own(r_ref[:, sl] * e_pos[g])).astype(BF16)
           for g, sl in enumerate(sls)]
    ab = each(_dot_nt, xar, yb)
    ak = each(_dot_nt, xar, yk)
    n = each(lambda t: jnp.where(strict, t[:W], 0.0), ab)
    a_k = each(lambda t: cat(jnp.where(strict, t[:W], 0.0), jnp.where(incl, t[W:], 0.0)).astype(BF16), ak)
    a_rb = each(lambda t: jnp.where(incl, t[W:], 0.0).astype(BF16), ab)
    s0 = [state_ref[g] for g in packs]
    xs = each(_dot_nt, xar, s0)
    av = each(_dot, a_k, vw)
    tinv = each(lambda t: eye + t, n)
    p = each(_dot, n, n)
    for _ in range(4):
        pt = each(lambda pp, tt: _dot(cat(pp, tt), pp), p, tinv)
        p = each(lambda t: t[:W], pt)
        tinv = each(lambda tt, t: tt + t[W:], tinv, pt)
    tinv = each(lambda tt, pp: tt + _dot(tt, pp), tinv, p)
    u = each(lambda tt, x, a: _dot(tt, x[:W] + a[:W]).astype(BF16), tinv, xs, av)
    yw = each(lambda x, a, arb, uu: x[W:] + a[W:] + _dot(arb, uu), xs, av, a_rb, u)
    upd = each(lambda uu, vv, b, k: _dot_tn(cat(uu, vv), cat(jnp.where(same, b, 0.0), jnp.where(same, k, 0.0))),
               u, vw, yb, yk)
    for g in packs:
        state_ref[g] = (s0[g] + upd[g]) * p_last[g]
        mean = jnp.sum(yw[g], axis=1, keepdims=True) * (1.0 / HD)
        d = jnp.where(same, yw[g] - mean, 0.0)
        var = jnp.sum(d * d, axis=1, keepdims=True) * (1.0 / HD)
        yn = d * lax.rsqrt(var + RW_GN_EPS)
        yn = functools.reduce(jnp.add, [yn[i * L:(i + 1) * L] for i in range(RW_PACK)])
        yn = yn * lnw_ref[:, sls[g]] + lnb_ref[:, sls[g]] + bonus_ref[:, sls[g]]
        y_ref[:, sls[g]] = (yn * _silu(g_ref[:, sls[g]])).astype(y_ref.dtype)


def _rwkv_group(proj, prm, seg, segt, tiles):
    s = proj.shape[0]
    g = D_GROUP
    tm = tiles.rw_prep
    row = lambda v: v.reshape(1, -1)
    wup = jnp.concatenate([prm["w_up"], jnp.zeros_like(prm["w_up"])], axis=0).astype(BF16)
    aup = jnp.concatenate([jnp.zeros_like(prm["a_up"]), prm["a_up"]], axis=0).astype(BF16)
    mu = prm["mu"]
    full = lambda shape: pl.BlockSpec(shape, lambda i: (0, 0))
    cur = lambda off, w: pl.BlockSpec((tm, w), lambda i: (i, off // w))
    prev = lambda off, w: pl.BlockSpec((8, w), lambda i: (jnp.maximum(i * (tm // 8) - 1, 0), off // w))
    fields = [(P_RW, g), (P_RW + g, g), (P_RW + 2 * g, g), (P_RW_LORA, LANES)]
    outs = pl.pallas_call(
        _rw_prep_kernel,
        grid=(s // tm,),
        in_specs=[cur(*f) for f in fields] + [prev(*f) for f in fields]
        + [full((1, g))] * 3 + [full((1, LANES)), full((1, g)), full((LANES, g)), full((1, g)), full((LANES, g)),
                                full((1, g)), full((1, g)), full((1, g)), full((g, LANES)), full((LANES, g))],
        out_specs=[pl.BlockSpec((tm, g), lambda i: (i, 0))] * 7,
        out_shape=[jax.ShapeDtypeStruct((s, g), dt) for dt in (BF16, F32, BF16, BF16, BF16, BF16, BF16)],
        compiler_params=_cparams("parallel"),
        name="rwkv_prep",
    )(*([proj] * 8), row(mu[:g]), row(mu[g:2 * g]), row(mu[2 * g:3 * g]), row(mu[3 * g:]),
      row(prm["w0"]), wup, row(prm["a0"]), aup, row(prm["k_k"]), row(prm["k_a"]), row(prm["r_k"]), seg, segt)
    r, lw, k2, v, kk, b, bonus = outs
    L = RW_CHUNK
    blk = pl.BlockSpec((L, g), lambda i: (i, 0))
    return pl.pallas_call(
        _rw_scan_kernel,
        grid=(s // L,),
        in_specs=[blk] * 7 + [pl.BlockSpec((L, g), lambda i: (i, P_GATE // g)), full((1, g)), full((1, g))],
        out_specs=blk,
        out_shape=jax.ShapeDtypeStruct((s, g), BF16),
        scratch_shapes=[pltpu.VMEM((N_HEADS // RW_PACK, RW_PACK * HD, RW_PACK * HD), F32)],
        compiler_params=_cparams("arbitrary"),
        name="rwkv_scan",
    )(r, lw, k2, v, kk, b, bonus, proj, row(prm["lnx_w"]), row(prm["lnx_b"]))


def _fx_prep_kernel(q_ref, k_ref, v_ref, f_ref, qw_ref, kw_ref, fb_ref, seg_ref, segt_ref, scat_ref,
                    qo_ref, ko_ref, vo_ref, carry_ref):
    t = q_ref.shape[0]

    @pl.when(pl.program_id(0) == 0)
    def _():
        carry_ref[...] = jnp.zeros_like(carry_ref)

    def headnorm(x, w):
        ss = jnp.dot((x * x).astype(BF16), seg_ref[...], preferred_element_type=F32)
        ms = jnp.dot(ss.astype(BF16), segt_ref[...], preferred_element_type=F32) * (1.0 / HD)
        return x * lax.rsqrt(ms + NORM_EPS) * w

    qn = headnorm(q_ref[...], qw_ref[...]) * (HD ** -0.5 * LOG2E)
    kn = headnorm(k_ref[...], kw_ref[...])
    z = f_ref[...] + fb_ref[...]
    log_f = jnp.minimum(z, 0.0) - jnp.log(1.0 + jnp.exp(-jnp.abs(z)))
    c = _dot_exact_lhs(_tri_incl(t).astype(BF16), log_f) + carry_ref[...]
    carry_ref[...] = c[t - 1:t, :]
    vt = v_ref[...].T.astype(vo_ref.dtype)
    k_aug_all = jnp.dot(jnp.concatenate(_split3(c * (-LOG2E)), axis=1), scat_ref[...], preferred_element_type=F32)
    lane = lax.broadcasted_iota(jnp.int32, (t, LANES), 1)
    for h in range(N_HEADS):
        pair = slice((h // 2) * LANES, (h // 2 + 1) * LANES)
        own = (lane >= HD) if h % 2 else (lane < HD)
        aug0 = 0 if h % 2 else HD
        q_aug = jnp.where(lane < aug0 + FX_AUG, 1.0, 0.0)
        ko_ref[h] = jnp.where(own, kn[:, pair], k_aug_all[:, h * LANES:(h + 1) * LANES]).astype(BF16)
        qo_ref[h] = jnp.where(own, qn[:, pair], q_aug).astype(BF16)
        vo_ref[h, :HD, :] = vt[h * HD:(h + 1) * HD, :]
        vo_ref[h, HD:, :] = jnp.ones((FX_VROWS - HD, t), vo_ref.dtype)


def _fx_attn_kernel(q_ref, k_ref, vt_ref, g_ref, o_ref, s_ref, *, tk):
    i = pl.program_id(1)
    tq = q_ref.shape[1]
    heads = range(q_ref.shape[0])
    q = [q_ref[e] for e in heads]
    assert tq == tk

    def rows(j):
        return pl.ds(pl.multiple_of(j * tk, tk), tk)

    def scores(j, buf):
        for e in heads:
            s_ref[buf, e] = lax.dot_general(k_ref[e, rows(j), :], q[e], (((1,), (1,)), ((), ())),
                                            preferred_element_type=F32)

    def absorb(j, buf, carry, masked):
        s = [s_ref[buf, e] for e in heads]
        if masked:
            visible = (lax.broadcasted_iota(jnp.int32, (tk, tq), 0) <= lax.broadcasted_iota(jnp.int32, (tk, tq), 1))
            s = [jnp.where(visible, t, -1e30) for t in s]
        m_new = [jnp.maximum(carry[e][0], jnp.max(s[e], axis=0, keepdims=True)) for e in heads]
        p = [jnp.exp2(s[e] - m_new[e]).astype(BF16) for e in heads]
        pv = [jnp.dot(vt_ref[e, :, rows(j)], p[e], preferred_element_type=F32) for e in heads]
        return tuple((m_new[e], jnp.exp2(carry[e][0] - m_new[e]) * carry[e][1] + pv[e]) for e in heads)

    def pair(jj, carry):
        scores(2 * jj + 1, 1)
        carry = absorb(2 * jj, 0, carry, False)
        scores(2 * jj + 2, 0)
        return absorb(2 * jj + 1, 1, carry, False)

    def finish(carry):
        ot = jnp.concatenate([carry[e][1][:HD] / carry[e][1][HD:HD + 1] for e in heads], axis=0)
        o_ref[...] = (ot.T * _silu(g_ref[...])).astype(o_ref.dtype)

    init = tuple((jnp.full((1, tq), -1e30, F32), jnp.zeros((FX_VROWS, tq), F32)) for _ in heads)
    scores(0, 0)
    carry = lax.fori_loop(0, i // 2, pair, init)

    @pl.when(i % 2 == 0)
    def _():
        finish(absorb(i, 0, carry, True))

    @pl.when(i % 2 == 1)
    def _():
        scores(i, 1)
        finish(absorb(i, 1, absorb(i - 1, 0, carry, False), True))


def _fox_group(proj, prm, seg, segt, tiles):
    s = proj.shape[0]
    g = D_GROUP
    tm, tq = tiles.prep, tiles.fox
    full = lambda shape: pl.BlockSpec(shape, lambda i: (0, 0))
    col = lambda off, w: pl.BlockSpec((tm, w), lambda i: (i, off // w))
    tile_w = lambda w: jnp.tile(w, N_HEADS).reshape(1, g)
    fb = jnp.zeros((1, LANES), F32).at[0, :N_HEADS].set(prm["f_bias"])
    heads = jnp.arange(N_HEADS)
    dst = heads * LANES + jnp.where(heads % 2 == 1, 0, HD)
    scat = jnp.zeros((FX_AUG * LANES, N_HEADS * LANES), BF16)
    for p in range(FX_AUG):
        scat = scat.at[p * LANES + heads, dst + p].set(1)
    q_aug, k_aug, vt_aug = pl.pallas_call(
        _fx_prep_kernel,
        grid=(s // tm,),
        in_specs=[col(P_FX, g), col(P_FX + g, g), col(P_FX + 2 * g, g), col(P_FX_F, LANES),
                  full((1, g)), full((1, g)), full((1, LANES)), full((g, LANES)), full((LANES, g)),
                  full((FX_AUG * LANES, N_HEADS * LANES))],
        out_specs=[pl.BlockSpec((N_HEADS, tm, LANES), lambda i: (0, i, 0)),
                   pl.BlockSpec((N_HEADS, tm, LANES), lambda i: (0, i, 0)),
                   pl.BlockSpec((N_HEADS, FX_VROWS, tm), lambda i: (0, 0, i))],
        out_shape=[jax.ShapeDtypeStruct((N_HEADS, s, LANES), BF16)] * 2
        + [jax.ShapeDtypeStruct((N_HEADS, FX_VROWS, s), BF16)],
        scratch_shapes=[pltpu.VMEM((1, LANES), F32)],
        compiler_params=_cparams("arbitrary"),
        name="fox_prep",
    )(proj, proj, proj, proj, tile_w(prm["q_norm_w"]), tile_w(prm["k_norm_w"]), fb, seg, segt, scat)
    nh = FX_HEADS_PER_STEP
    wo = nh * HD
    return pl.pallas_call(
        functools.partial(_fx_attn_kernel, tk=tq),
        grid=(N_HEADS // nh, s // tq),
        in_specs=[pl.BlockSpec((nh, tq, LANES), lambda j, i: (j, i, 0)),
                  pl.BlockSpec((nh, s, LANES), lambda j, i: (j, 0, 0)),
                  pl.BlockSpec((nh, FX_VROWS, s), lambda j, i: (j, 0, 0)),
                  pl.BlockSpec((tq, wo), lambda j, i: (i, (P_GATE + g) // wo + j))],
        out_specs=pl.BlockSpec((tq, wo), lambda j, i: (i, j)),
        out_shape=jax.ShapeDtypeStruct((s, g), BF16),
        scratch_shapes=[pltpu.VMEM((2, nh, tq, tq), F32)],
        compiler_params=_cparams("parallel", "parallel"),
        name="fox_attn",
    )(q_aug, k_aug, vt_aug, proj)


def _ssd_kernel(xbc_ref, dt_ref, z_ref, cw_ref, cb_ref, dtb_ref, alog_ref, dskip_ref, nw_ref, segt_ref,
                o_ref, tail_ref, state_ref):
    t = xbc_ref.shape[0]
    g = D_GROUP
    n = SSM_STATE

    @pl.when(pl.program_id(0) == 0)
    def _():
        tail_ref[:8, :] = jnp.zeros((8, tail_ref.shape[1]), F32)
        state_ref[...] = jnp.zeros_like(state_ref)

    cur = xbc_ref[...]
    tail_ref[8:, :] = cur
    conv = cur * cw_ref[SSM_CONV - 1:SSM_CONV, :] + cb_ref[...]
    for sft in range(1, SSM_CONV):
        conv = conv + tail_ref[8 - sft:8 - sft + t, :] * cw_ref[SSM_CONV - 1 - sft:SSM_CONV - sft, :]
    tail_ref[:8, :] = cur[t - 8:, :]
    xbc = _silu(conv)
    xs = xbc[:, :g]
    dt = _softplus(dt_ref[...] + dtb_ref[...])
    da = dt * (-jnp.exp(alog_ref[...]))
    tri = _tri_incl(t)
    acs = _dot_exact_lhs(tri.astype(BF16), da)
    acs2 = acs * LOG2E
    acs2_t = acs2.T
    e_acs = jnp.exp(acs)
    tile_decay = e_acs[t - 1:t, :]
    xdt = xs * _expand_heads(dt, segt_ref)
    xdec = xdt * _expand_heads(jnp.exp(acs[t - 1:t, :] - acs), segt_ref)
    y_diag, y_off = [], []
    for grp in range(SSM_GROUPS):
        bm = xbc[:, g + grp * n:g + (grp + 1) * n]
        cm = xbc[:, g + SSM_GROUPS * n + grp * n:g + SSM_GROUPS * n + (grp + 1) * n]
        cb = _dot_nt(cm, bm)
        for hh in range(N_HEADS // SSM_GROUPS):
            h = grp * (N_HEADS // SSM_GROUPS) + hh
            hs = slice(h * HD, (h + 1) * HD)
            lmat = jnp.exp2(jnp.where(tri, acs2[:, h:h + 1] - acs2_t[h:h + 1, :], -jnp.inf))
            st = state_ref[h]
            y_diag.append(_dot(cb * lmat, xdt[:, hs]))
            y_off.append(_dot_nt(cm, st))
            state_ref[h] = st * tile_decay[:, h:h + 1] + _dot_tn(xdec[:, hs], bm)
    y = jnp.concatenate(y_diag, axis=1) + jnp.concatenate(y_off, axis=1) * _expand_heads(e_acs, segt_ref)
    y = (y + dskip_ref[...] * xs) * _silu(z_ref[...])
    gw = g // SSM_GROUPS
    outs = []
    for grp in range(SSM_GROUPS):
        yg = y[:, grp * gw:(grp + 1) * gw]
        outs.append(yg * lax.rsqrt(jnp.mean(yg * yg, axis=-1, keepdims=True) + NORM_EPS))
    o_ref[...] = (jnp.concatenate(outs, axis=1) * nw_ref[...]).astype(o_ref.dtype)


def _ssd_group(proj, prm, segt, tiles):
    s = proj.shape[0]
    g = D_GROUP
    t = tiles.ssd
    full = lambda shape: pl.BlockSpec(shape, lambda i: (0, 0))
    pad_heads = lambda v: jnp.zeros((1, LANES), F32).at[0, :N_HEADS].set(v)
    return pl.pallas_call(
        _ssd_kernel,
        grid=(s // t,),
        in_specs=[pl.BlockSpec((t, SSM_CONV_DIM), lambda i: (i, P_SSM // SSM_CONV_DIM)),
                  pl.BlockSpec((t, LANES), lambda i: (i, P_SSM_DT // LANES)),
                  pl.BlockSpec((t, g), lambda i: (i, (P_GATE + 2 * g) // g)),
                  full((SSM_CONV, SSM_CONV_DIM)), full((1, SSM_CONV_DIM)), full((1, LANES)), full((1, LANES)),
                  full((1, g)), full((1, g)), full((LANES, g))],
        out_specs=pl.BlockSpec((t, g), lambda i: (i, 0)),
        out_shape=jax.ShapeDtypeStruct((s, g), BF16),
        scratch_shapes=[pltpu.VMEM((8 + t, SSM_CONV_DIM), F32), pltpu.VMEM((N_HEADS, HD, SSM_STATE), F32)],
        compiler_params=_cparams("arbitrary"),
        name="ssd_scan",
    )(proj, proj, proj, prm["conv_w"], prm["conv_b"].reshape(1, -1), pad_heads(prm["dt_bias"]),
      pad_heads(prm["a_log"]), jnp.repeat(prm["d"], HD).reshape(1, g), prm["norm_w"].reshape(1, -1), segt)


def _mem_attn_kernel(q_ref, g_ref, kv_ref, qw_ref, kw_ref, o_ref):
    outs = []
    for h in range(MEM_HEADS):
        sl = slice(h * MEM_HD, (h + 1) * MEM_HD)

        def norm(x, w):
            return x * lax.rsqrt(jnp.mean(x * x, axis=-1, keepdims=True) + NORM_EPS) * w

        q = norm(q_ref[:, sl], qw_ref[...]) * (MEM_HD ** -0.5)
        k = norm(kv_ref[:, sl], kw_ref[...])
        v = kv_ref[:, D_GROUP + h * MEM_HD:D_GROUP + (h + 1) * MEM_HD]
        s = _dot_nt(q, k)
        p = jnp.exp(s - jnp.max(s, axis=-1, keepdims=True))
        outs.append(_dot(p, v) / jnp.sum(p, axis=-1, keepdims=True))
    o_ref[...] = (jnp.concatenate(outs, axis=1) * _silu(g_ref[...])).astype(o_ref.dtype)


def _mem_group(proj, mem2d, prm, tiles):
    s = proj.shape[0]
    g = D_GROUP
    tm = tiles.proj_m
    n_mem = mem2d.shape[0]
    kv = _norm_matmul(mem2d, prm["norm_w_all"], prm["w_kv_all"], n_mem, tiles.kv_n, layer=prm["layer"])
    full = lambda shape: pl.BlockSpec(shape, lambda i: (0, 0))
    return pl.pallas_call(
        _mem_attn_kernel,
        grid=(s // tm,),
        in_specs=[pl.BlockSpec((tm, g), lambda i: (i, P_QMEM // g)),
                  pl.BlockSpec((tm, g), lambda i: (i, (P_GATE + 3 * g) // g)),
                  full((n_mem, 2 * g)), full((1, MEM_HD)), full((1, MEM_HD))],
        out_specs=pl.BlockSpec((tm, g), lambda i: (i, 0)),
        out_shape=jax.ShapeDtypeStruct((s, g), BF16),
        compiler_params=_cparams("parallel"),
        name="mem_attn",
    )(proj, proj, kv, prm["q_norm_w"].reshape(1, -1), prm["k_norm_w"].reshape(1, -1))


def _layer(x, mem2d, lp, seg, segt):
    tiles = _tiles(x.shape[0])
    w_packed = _pack_w_in(lp["w_in_t_all"], lp["layer"], tiles.pack_rows)
    proj = _norm_matmul(x, lp["norm_w"], w_packed, tiles.proj_m, tiles.proj_n)
    y_rw = _rwkv_group(proj, lp["rw"], seg, segt, tiles)
    y_fx = _fox_group(proj, lp["fx"], seg, segt, tiles)
    y_ssm = _ssd_group(proj, lp["ssm"], segt, tiles)
    y_mem = _mem_group(proj, mem2d, lp["mem"], tiles)
    return _out_proj(x, (y_rw, y_fx, y_ssm, y_mem), lp["w_out_all"], lp["layer"], tiles.proj_m, tiles.out_n)


def kernel(x, mem, norm_w, w_in, rw_mu, rw_w0, rw_w_up, rw_a0, rw_a_up, rw_k_k, rw_k_a, rw_r_k, rw_lnx_w, rw_lnx_b, fx_f_bias, fx_q_norm_w, fx_k_norm_w, ssm_conv_w, ssm_conv_b, ssm_dt_bias, ssm_a_log, ssm_d, ssm_norm_w, mem_norm_w, mem_w_kv, mem_q_norm_w, mem_k_norm_w, w_out):
    assert x.shape[0] == 1 and mem.shape[0] == 1
    x2 = x[0]
    mem2d = mem[0]
    seg = (jnp.arange(D_GROUP)[:, None] // HD == jnp.arange(LANES)[None, :]).astype(BF16)
    segt = seg.T
    w_in_t = jnp.swapaxes(w_in, 1, 2)
    for l in range(DEPTH):
        lp = {
            "layer": l, "norm_w": norm_w[l], "w_in_t_all": w_in_t, "w_out_all": w_out,
            "rw": {"mu": rw_mu[l], "w0": rw_w0[l], "w_up": rw_w_up[l], "a0": rw_a0[l], "a_up": rw_a_up[l],
                   "k_k": rw_k_k[l], "k_a": rw_k_a[l], "r_k": rw_r_k[l], "lnx_w": rw_lnx_w[l], "lnx_b": rw_lnx_b[l]},
            "fx": {"f_bias": fx_f_bias[l], "q_norm_w": fx_q_norm_w[l], "k_norm_w": fx_k_norm_w[l]},
            "ssm": {"conv_w": ssm_conv_w[l], "conv_b": ssm_conv_b[l], "dt_bias": ssm_dt_bias[l], "a_log": ssm_a_log[l],
                    "d": ssm_d[l], "norm_w": ssm_norm_w[l]},
            "mem": {"layer": l, "norm_w_all": mem_norm_w, "w_kv_all": mem_w_kv, "q_norm_w": mem_q_norm_w[l],
                    "k_norm_w": mem_k_norm_w[l]},
        }
        x2 = _layer(x2, mem2d, lp, seg, segt)
    return x2[None]
```
